```python
import jax
import jax.numpy as jnp
from jax import lax
import numpy as np

D_MODEL = 4096
BATCH = 1
SEQ = 8192
DEPTH = 4

GRID_W = 64
CTX_LEN = 256
HEAD_DIM = 128
N_Q_HEADS = 8
N_KV_HEADS = 2
GQA_GROUP = N_Q_HEADS // N_KV_HEADS
Q_W = N_Q_HEADS * HEAD_DIM
KV_W = N_KV_HEADS * HEAD_DIM
WINDOW = 128
ATTN_BLOCK = 128
ROPE_THETA = 10000.0
SGU_GROUPS = 8
SGU_CH = 128
SGU_W = SGU_GROUPS * SGU_CH
CHUNK = 128
CONV_W = 1024
CONV_K = 3
FOURIER_GROUPS = 4
FOURIER_CH = 256
FOURIER_W = FOURIER_GROUPS * FOURIER_CH
N_BRANCH = 4
BRANCH_W = 1024
D_FF = 3584
ADA_RANK = 256
N_MOD = 9
NORM_EPS = 1e-6
MASK_VALUE = -1e30
_COL_SIZES = (Q_W, KV_W, KV_W, SGU_W, SGU_W, CONV_W, CONV_W, CONV_W, FOURIER_W, N_BRANCH * D_MODEL)
IN_COLS = sum(_COL_SIZES)

kernel_name = 'hybrid_flow_prefix_trunk'


def rms_norm(x, g):
    x32 = x.astype(jnp.float32)
    y = x32 * lax.rsqrt(jnp.mean(x32 * x32, axis=-1, keepdims=True) + NORM_EPS)
    return (y * g.astype(jnp.float32)).astype(x.dtype)


def modulate(xn, shift, scale):
    return xn * (1.0 + scale[:, None, :]) + shift[:, None, :]


def ada_mod(cvec, down, up, bias):
    h = (jax.nn.silu(cvec) @ down) @ up + bias
    return h.reshape(cvec.shape[0], N_MOD, D_MODEL)


def mod_in(h, g, m, i):
    return modulate(rms_norm(h, g), m[:, 3 * i], m[:, 3 * i + 1])


def mod_gate(m, i):
    return m[:, 3 * i + 2][:, None, :]


def swiglu(z, wi, wo):
    a, b = jnp.split(z @ wi, 2, axis=-1)
    return (jax.nn.silu(a) * b) @ wo


def split_projection(p):
    offs, acc = [], 0
    for s in _COL_SIZES[:-1]:
        acc += s
        offs.append(acc)
    return jnp.split(p, offs, axis=-1)


def to_heads(t, n_heads):
    return t.reshape(t.shape[:-1] + (n_heads, HEAD_DIM))


def axial_rope_tables(rows):
    n = rows * GRID_W
    row = jnp.repeat(jnp.arange(rows), GRID_W).astype(jnp.float32)
    col = (jnp.arange(n) % GRID_W).astype(jnp.float32)
    ax = HEAD_DIM // 2
    inv_freq = 1.0 / (ROPE_THETA ** (jnp.arange(0, ax, 2, dtype=jnp.float32) / ax))
    ang_r = row[:, None] * inv_freq[None, :]
    ang_c = col[:, None] * inv_freq[None, :]
    emb = jnp.concatenate([ang_r, ang_r, ang_c, ang_c], axis=-1)
    return jnp.cos(emb), jnp.sin(emb)


def apply_rope(t, cos, sin):
    t32 = t.astype(jnp.float32)
    tr = t32.reshape(t.shape[:-1] + (2, 2, HEAD_DIM // 4))
    rot = jnp.stack([-tr[..., 1, :], tr[..., 0, :]], axis=-2).reshape(t.shape)
    return (t32 * cos[:, None, :] + rot * sin[:, None, :]).astype(t.dtype)


def sink_column(sink, shape):
    s = sink.astype(jnp.float32).reshape(N_KV_HEADS, GQA_GROUP)[:, :, None, None]
    return jnp.broadcast_to(s, shape)


def window_attention(q, k, v, kc, vc, sink):
    b, n = q.shape[0], q.shape[1]
    nb = n // ATTN_BLOCK
    n_loc = 3 * ATTN_BLOCK
    n_ctx = kc.shape[1]
    scale = HEAD_DIM ** -0.5
    qb = q.reshape(b, nb, ATTN_BLOCK, N_KV_HEADS, GQA_GROUP, HEAD_DIM)

    def band(t):
        tp = jnp.pad(t, ((0, 0), (ATTN_BLOCK, ATTN_BLOCK), (0, 0), (0, 0)))
        tb = tp.reshape(b, nb + 2, ATTN_BLOCK, N_KV_HEADS, HEAD_DIM)
        return jnp.concatenate([tb[:, :-2], tb[:, 1:-1], tb[:, 2:]], axis=2)

    kw, vw = band(k), band(v)
    s_loc = jnp.einsum('bnqhgd,bnkhd->bnhgqk', qb, kw).astype(jnp.float32) * scale
    blk = jnp.arange(nb)[:, None]
    qpos = blk * ATTN_BLOCK + jnp.arange(ATTN_BLOCK)[None, :]
    kpos = (blk - 1) * ATTN_BLOCK + jnp.arange(n_loc)[None, :]
    dist = kpos[:, None, :] - qpos[:, :, None]
    valid = (jnp.abs(dist) <= WINDOW) & (kpos[:, None, :] >= 0) & (kpos[:, None, :] < n)
    s_loc = jnp.where(valid[None, :, None, None], s_loc, MASK_VALUE)
    s_ctx = jnp.einsum('bnqhgd,blhd->bnhgql', qb, kc).astype(jnp.float32) * scale
    s_sink = sink_column(sink, s_loc.shape[:-1] + (1,))
    p = jax.nn.softmax(jnp.concatenate([s_loc, s_ctx, s_sink], axis=-1), axis=-1).astype(v.dtype)
    o = (jnp.einsum('bnhgqk,bnkhd->bnqhgd', p[..., :n_loc], vw)
         + jnp.einsum('bnhgql,blhd->bnqhgd', p[..., n_loc:n_loc + n_ctx], vc))
    return o.reshape(b, n, Q_W)


def context_attention(qc, kc, vc, sink):
    b, L = qc.shape[0], qc.shape[1]
    qg = qc.reshape(b, L, N_KV_HEADS, GQA_GROUP, HEAD_DIM)
    s = jnp.einsum('blhgd,bmhd->bhglm', qg, kc).astype(jnp.float32) * (HEAD_DIM ** -0.5)
    s_sink = sink_column(sink, s.shape[:-1] + (1,))
    p = jax.nn.softmax(jnp.concatenate([s, s_sink], axis=-1), axis=-1).astype(vc.dtype)
    o = jnp.einsum('bhglm,bmhd->blhgd', p[..., :L], vc)
    return o.reshape(b, L, Q_W)


def spatial_gating(u, v, ln_g, ln_b, w_s, b_s):
    b, t = u.shape[0], u.shape[1]
    vg = jax.nn.gelu(v).reshape(b, t, SGU_GROUPS, SGU_CH).astype(jnp.float32)
    mu = jnp.mean(vg, axis=-1, keepdims=True)
    var = jnp.mean(jnp.square(vg - mu), axis=-1, keepdims=True)
    vn = ((vg - mu) * lax.rsqrt(var + NORM_EPS) * ln_g.reshape(SGU_GROUPS, SGU_CH).astype(jnp.float32)
          + ln_b.reshape(SGU_GROUPS, SGU_CH).astype(jnp.float32))
    vn = vn.astype(v.dtype).reshape(b, t // CHUNK, CHUNK, SGU_GROUPS, SGU_CH)
    mixed = jnp.einsum('gpq,bcqgd->bcpgd', w_s, vn) + b_s.T[:, :, None]
    return jax.nn.gelu(u) * mixed.reshape(b, t, SGU_W)


def short_conv(xi, gate_b, gate_c, w):
    z = gate_c * xi
    y = lax.conv_general_dilated(z, w[:, None, :], window_strides=(1,),
                                 padding=((CONV_K // 2, CONV_K // 2),),
                                 dimension_numbers=('NWC', 'WIO', 'NWC'),
                                 feature_group_count=CONV_W)
    return gate_b * y


def fourier_mix(f):
    b, t = f.shape[0], f.shape[1]
    fg = f.reshape(b, t, FOURIER_GROUPS, FOURIER_CH).astype(jnp.float32)
    y = jnp.fft.fft2(fg, axes=(1, 3), norm='ortho').real
    return y.astype(f.dtype).reshape(b, t, FOURIER_W)


def merge_branches(ys, gate_logits, w_branch, w_out):
    gates = jax.nn.sigmoid(gate_logits.reshape(gate_logits.shape[:-1] + (N_BRANCH, D_MODEL)))
    merged = gates[..., 0, :] * (ys[0] @ w_branch[0])
    for r in range(1, N_BRANCH):
        merged = merged + gates[..., r, :] * (ys[r] @ w_branch[r])
    return merged @ w_out


def mixer(zl, zc, w_in, q_g, k_g, sink, sgu_ln_g, sgu_ln_b, sgu_w, sgu_b, conv_w, w_branch, w_out,
          cos, sin, context_out):
    ql, kl, vl, ul, vgl, cxl, cbl, ccl, fl, gl = split_projection(zl @ w_in)
    qc, kc, vc, uc, vgc, cxc, cbc, ccc, fc, gc = split_projection(zc @ w_in)
    kc_h = rms_norm(to_heads(kc, N_KV_HEADS), k_g)
    vc_h = to_heads(vc, N_KV_HEADS)
    ql_h = apply_rope(rms_norm(to_heads(ql, N_Q_HEADS), q_g), cos, sin)
    kl_h = apply_rope(rms_norm(to_heads(kl, N_KV_HEADS), k_g), cos, sin)
    ys_l = (window_attention(ql_h, kl_h, to_heads(vl, N_KV_HEADS), kc_h, vc_h, sink),
            spatial_gating(ul, vgl, sgu_ln_g, sgu_ln_b, sgu_w, sgu_b),
            short_conv(cxl, cbl, ccl, conv_w),
            fourier_mix(fl))
    out_l = merge_branches(ys_l, gl, w_branch, w_out)
    if not context_out:
        return out_l, None
    qc_h = rms_norm(to_heads(qc, N_Q_HEADS), q_g)
    ys_c = (context_attention(qc_h, kc_h, vc_h, sink),
            spatial_gating(uc, vgc, sgu_ln_g, sgu_ln_b, sgu_w, sgu_b),
            short_conv(cxc, cbc, ccc, conv_w),
            fourier_mix(fc))
    out_c = merge_branches(ys_c, gc, w_branch, w_out)
    return out_l, out_c


def setup_inputs(seed: int = 0) -> dict:
    key = jax.random.key(seed)
    ks = jax.random.split(key, 21)
    f32 = jnp.float32

    def nrm(k, shape, s):
        return jax.random.normal(k, shape, f32) * s

    return {
        'x': nrm(ks[0], (BATCH, SEQ, D_MODEL), 1.0),
        'c': nrm(ks[1], (BATCH, D_MODEL), 1.0),
        'ctx': nrm(ks[2], (BATCH, CTX_LEN, D_MODEL), 1.0),
        'c_ctx': nrm(ks[3], (D_MODEL,), 1.0),
        'ada_down': nrm(ks[4], (DEPTH, D_MODEL, ADA_RANK), D_MODEL ** -0.5),
        'ada_up': nrm(ks[5], (DEPTH, ADA_RANK, N_MOD * D_MODEL), 0.5 * ADA_RANK ** -0.5),
        'ada_b': nrm(ks[6], (DEPTH, N_MOD * D_MODEL), 0.02),
        'norm_g': 1.0 + nrm(ks[7], (DEPTH, 3, D_MODEL), 0.02),
        'ffn_wi': nrm(ks[8], (DEPTH, 2, D_MODEL, 2 * D_FF), D_MODEL ** -0.5),
        'ffn_wo': nrm(ks[9], (DEPTH, 2, D_FF, D_MODEL), D_FF ** -0.5),
        'w_in': nrm(ks[10], (DEPTH, D_MODEL, IN_COLS), D_MODEL ** -0.5),
        'q_norm': 1.0 + nrm(ks[11], (DEPTH, HEAD_DIM), 0.02),
        'k_norm': 1.0 + nrm(ks[12], (DEPTH, HEAD_DIM), 0.02),
        'sink': nrm(ks[13], (DEPTH, N_Q_HEADS), 0.5),
        'sgu_ln_g': 1.0 + nrm(ks[14], (DEPTH, SGU_W), 0.02),
        'sgu_ln_b': nrm(ks[15], (DEPTH, SGU_W), 0.02),
        'sgu_w': nrm(ks[16], (DEPTH, SGU_GROUPS, CHUNK, CHUNK), CHUNK ** -0.5),
        'sgu_b': nrm(ks[17], (DEPTH, SGU_GROUPS, CHUNK), 0.02),
        'conv_w': nrm(ks[18], (DEPTH, CONV_K, CONV_W), CONV_K ** -0.5),
        'w_branch': nrm(ks[19], (DEPTH, N_BRANCH, BRANCH_W, D_MODEL), BRANCH_W ** -0.5),
        'w_out': nrm(ks[20], (DEPTH, D_MODEL, D_MODEL), D_MODEL ** -0.5),
    }


def reference(x, c, ctx, c_ctx, ada_down, ada_up, ada_b, norm_g, ffn_wi, ffn_wo, w_in, q_norm, k_norm,
              sink, sgu_ln_g, sgu_ln_b, sgu_w, sgu_b, conv_w, w_branch, w_out):
    n = x.shape[1]
    rows = n // GRID_W
    cos, sin = axial_rope_tables(rows)
    h, hc = x, ctx
    for l in range(DEPTH):
        last = l == DEPTH - 1
        m = ada_mod(c, ada_down[l], ada_up[l], ada_b[l])
        mc = ada_mod(c_ctx[None, :], ada_down[l], ada_up[l], ada_b[l])
        h = h + 0.5 * mod_gate(m, 0) * swiglu(mod_in(h, norm_g[l, 0], m, 0), ffn_wi[l, 0], ffn_wo[l, 0])
        hc = hc + 0.5 * mod_gate(mc, 0) * swiglu(mod_in(hc, norm_g[l, 0], mc, 0), ffn_wi[l, 0], ffn_wo[l, 0])
        yl, yc = mixer(mod_in(h, norm_g[l, 1], m, 1), mod_in(hc, norm_g[l, 1], mc, 1), w_in[l],
                       q_norm[l], k_norm[l], sink[l], sgu_ln_g[l], sgu_ln_b[l], sgu_w[l], sgu_b[l],
                       conv_w[l], w_branch[l], w_out[l], cos, sin, not last)
        h = h + mod_gate(m, 1) * yl
        if not last:
            hc = hc + mod_gate(mc, 1) * yc
            hc = hc + 0.5 * mod_gate(mc, 2) * swiglu(mod_in(hc, norm_g[l, 2], mc, 2), ffn_wi[l, 1], ffn_wo[l, 1])
        h = h + 0.5 * mod_gate(m, 2) * swiglu(mod_in(h, norm_g[l, 2], m, 2), ffn_wi[l, 1], ffn_wo[l, 1])
    return h
```

```python
import functools

import numpy as np
import jax
import jax.numpy as jnp
from jax import lax
from jax.experimental import pallas as pl
from jax.experimental.pallas import tpu as pltpu

D_MODEL = 4096
SEQ = 8192
DEPTH = 4
GRID_W = 64
CTX_LEN = 256
S_ALL = SEQ + CTX_LEN
HEAD_DIM = 128
N_Q_HEADS = 8
N_KV_HEADS = 2
GQA_GROUP = N_Q_HEADS // N_KV_HEADS
Q_W = N_Q_HEADS * HEAD_DIM
KV_W = N_KV_HEADS * HEAD_DIM
WINDOW = 128
ATTN_BLOCK = 128
ROPE_THETA = 10000.0
SGU_GROUPS = 8
SGU_CH = 128
SGU_W = SGU_GROUPS * SGU_CH
CHUNK = 128
CONV_W = 1024
FOURIER_GROUPS = 4
FOURIER_CH = 256
FOURIER_W = FOURIER_GROUPS * FOURIER_CH
N_BRANCH = 4
BRANCH_W = 1024
D_FF = 3584
ADA_RANK = 256
N_MOD = 9
NORM_EPS = 1e-6
MASK_VALUE = -1e30

COL_Q = 0
COL_K = Q_W
COL_V = COL_K + KV_W
COL_SGU_U = COL_V + KV_W
COL_SGU_V = COL_SGU_U + SGU_W
COL_CONV_X = COL_SGU_V + SGU_W
COL_CONV_B = COL_CONV_X + CONV_W
COL_CONV_C = COL_CONV_B + CONV_W
COL_FOURIER = COL_CONV_C + CONV_W
COL_GATES = COL_FOURIER + FOURIER_W
IN_COLS = COL_GATES + N_BRANCH * D_MODEL

BM = 768
BR = 256
DFT_T1 = 64
DFT_T2 = 128
MIB = 1024 * 1024

F32 = jnp.float32
BF16 = jnp.bfloat16


def _params(sem, vmem_mib):
    return pltpu.CompilerParams(dimension_semantics=sem, vmem_limit_bytes=vmem_mib * MIB)


def _dot(a, b):
    return jnp.dot(a, b, preferred_element_type=F32)


def _dot_nt(a, b):
    return lax.dot_general(a, b, (((1,), (1,)), ((), ())), preferred_element_type=F32)


def _ada_down_body(cv_ref, down_ref, o_ref):
    cv = cv_ref[...]
    s = (cv * jax.nn.sigmoid(cv)).astype(BF16)
    o_ref[...] = _dot(s, down_ref[...].astype(BF16))


def _ada_up_body(t_ref, up_ref, b_ref, o_ref):
    o_ref[...] = _dot(t_ref[...].astype(BF16), up_ref[...].astype(BF16)) + b_ref[...]


def _ada_mod(cv, ada_down, ada_up, ada_b):
    t = pl.pallas_call(
        _ada_down_body,
        grid=(DEPTH,),
        in_specs=[pl.BlockSpec((8, D_MODEL), lambda l: (0, 0)),
                  pl.BlockSpec((None, D_MODEL, ADA_RANK), lambda l: (l, 0, 0))],
        out_specs=pl.BlockSpec((None, 8, ADA_RANK), lambda l: (l, 0, 0)),
        out_shape=jax.ShapeDtypeStruct((DEPTH, 8, ADA_RANK), F32),
        compiler_params=_params(("arbitrary",), 32),
        name="ada_down",
    )(cv, ada_down)
    bn = 4608
    nj = N_MOD * D_MODEL // bn
    return pl.pallas_call(
        _ada_up_body,
        grid=(DEPTH, nj),
        in_specs=[pl.BlockSpec((None, 8, ADA_RANK), lambda l, j: (l, 0, 0)),
                  pl.BlockSpec((None, ADA_RANK, bn), lambda l, j: (l, 0, j)),
                  pl.BlockSpec((None, 1, bn), lambda l, j: (l, 0, j))],
        out_specs=pl.BlockSpec((None, 8, bn), lambda l, j: (l, 0, j)),
        out_shape=jax.ShapeDtypeStruct((DEPTH, 8, N_MOD * D_MODEL), F32),
        compiler_params=_params(("arbitrary", "arbitrary"), 32),
        name="ada_up",
    )(t, ada_up, ada_b.reshape(DEPTH, 1, N_MOD * D_MODEL))


def _norm_mod_body(h_ref, g_ref, shift_ref, scale_ref, o_ref):
    is_ctx = pl.program_id(0) >= SEQ // BR
    x = h_ref[...]
    r = lax.rsqrt(jnp.mean(x * x, axis=-1, keepdims=True) + NORM_EPS)
    scale = jnp.where(is_ctx, scale_ref[1:2, :], scale_ref[0:1, :])
    shift = jnp.where(is_ctx, shift_ref[1:2, :], shift_ref[0:1, :])
    y = (x * r) * g_ref[...]
    o_ref[...] = (y * (1.0 + scale) + shift).astype(o_ref.dtype)


def _norm_mod(h, g_row, m, i_mod):
    return pl.pallas_call(
        _norm_mod_body,
        grid=(S_ALL // BR,),
        in_specs=[pl.BlockSpec((BR, D_MODEL), lambda i: (i, 0)),
                  pl.BlockSpec((1, D_MODEL), lambda i: (0, 0)),
                  pl.BlockSpec((8, D_MODEL), lambda i: (0, 3 * i_mod)),
                  pl.BlockSpec((8, D_MODEL), lambda i: (0, 3 * i_mod + 1))],
        out_specs=pl.BlockSpec((BR, D_MODEL), lambda i: (i, 0)),
        out_shape=jax.ShapeDtypeStruct((S_ALL, D_MODEL), BF16),
        compiler_params=_params(("arbitrary",), 32),
        name="norm_mod",
    )(h, g_row, m, m)


def _swiglu_body(z_ref, wa_ref, wb_ref, o_ref):
    z = z_ref[...]
    a = _dot(z, wa_ref[...])
    b = _dot(z, wb_ref[...])
    o_ref[...] = (a * jax.nn.sigmoid(a) * b).astype(o_ref.dtype)


def _mm_swiglu(z, wi):
    bn = 512
    nj = D_FF // bn
    return pl.pallas_call(
        _swiglu_body,
        grid=(S_ALL // BM, nj),
        in_specs=[pl.BlockSpec((BM, D_MODEL), lambda i, j: (i, 0)),
                  pl.BlockSpec((D_MODEL, bn), lambda i, j: (0, j)),
                  pl.BlockSpec((D_MODEL, bn), lambda i, j: (0, j + nj))],
        out_specs=pl.BlockSpec((BM, bn), lambda i, j: (i, j)),
        out_shape=jax.ShapeDtypeStruct((S_ALL, D_FF), BF16),
        compiler_params=_params(("arbitrary", "arbitrary"), 48),
        name="mm_swiglu",
    )(z, wi, wi)


def _resid_body(x_ref, w_ref, h_ref, gate_ref, o_ref, *, coef):
    y = _dot(x_ref[...], w_ref[...])
    row = pl.program_id(0) * BM + lax.broadcasted_iota(jnp.int32, (BM, 1), 0)
    gate = jnp.where(row >= SEQ, gate_ref[1:2, :], gate_ref[0:1, :])
    o_ref[...] = h_ref[...] + (coef * gate) * y


def _mm_resid(xin, w, h, m, i_mod, coef):
    k = xin.shape[1]
    bn = 512
    nj = D_MODEL // bn
    return pl.pallas_call(
        functools.partial(_resid_body, coef=coef),
        grid=(S_ALL // BM, nj),
        in_specs=[pl.BlockSpec((BM, k), lambda i, j: (i, 0)),
                  pl.BlockSpec((k, bn), lambda i, j: (0, j)),
                  pl.BlockSpec((BM, bn), lambda i, j: (i, j)),
                  pl.BlockSpec((8, bn), lambda i, j: (0, (3 * i_mod + 2) * nj + j))],
        out_specs=pl.BlockSpec((BM, bn), lambda i, j: (i, j)),
        out_shape=jax.ShapeDtypeStruct((S_ALL, D_MODEL), F32),
        compiler_params=_params(("arbitrary", "arbitrary"), 48),
        name="mm_resid",
    )(xin, w, h, m)


def _plain_body(z_ref, w_ref, o_ref):
    o_ref[...] = _dot(z_ref[...], w_ref[...]).astype(o_ref.dtype)


def _mm_cols(z, w, col0, ncols):
    bn = 512
    j0 = col0 // bn
    return pl.pallas_call(
        _plain_body,
        grid=(S_ALL // BM, ncols // bn),
        in_specs=[pl.BlockSpec((BM, D_MODEL), lambda i, j: (i, 0)),
                  pl.BlockSpec((D_MODEL, bn), lambda i, j: (0, j0 + j))],
        out_specs=pl.BlockSpec((BM, bn), lambda i, j: (i, j)),
        out_shape=jax.ShapeDtypeStruct((S_ALL, ncols), BF16),
        compiler_params=_params(("arbitrary", "arbitrary"), 48),
        name="mm_cols",
    )(z, w)


def _merge_body(z_ref, y0, y1, y2, y3, g0, g1, g2, g3, b0, b1, b2, b3, o_ref):
    z = z_ref[...]
    acc = None
    for y_ref, g_ref, b_ref in ((y0, g0, b0), (y1, g1, b1), (y2, g2, b2), (y3, g3, b3)):
        t = jax.nn.sigmoid(_dot(z, g_ref[...])) * _dot(y_ref[...], b_ref[...])
        acc = t if acc is None else acc + t
    o_ref[...] = acc.astype(o_ref.dtype)


def _merge(z, ys, w_in, w_branch):
    bn = 256
    nj = D_MODEL // bn
    jg = COL_GATES // bn
    z_spec = pl.BlockSpec((BM, D_MODEL), lambda i, j: (i, 0))
    y_spec = pl.BlockSpec((BM, BRANCH_W), lambda i, j: (i, 0))
    g_specs = [pl.BlockSpec((D_MODEL, bn), functools.partial(lambda i, j, r: (0, jg + r * nj + j), r=r))
               for r in range(N_BRANCH)]
    b_specs = [pl.BlockSpec((BRANCH_W, bn), functools.partial(lambda i, j, r: (r, j), r=r))
               for r in range(N_BRANCH)]
    return pl.pallas_call(
        _merge_body,
        grid=(S_ALL // BM, nj),
        in_specs=[z_spec] + [y_spec] * N_BRANCH + g_specs + b_specs,
        out_specs=pl.BlockSpec((BM, bn), lambda i, j: (i, j)),
        out_shape=jax.ShapeDtypeStruct((S_ALL, D_MODEL), BF16),
        compiler_params=_params(("arbitrary", "arbitrary"), 56),
        name="merge",
    )(z, *ys, *([w_in] * N_BRANCH), *([w_branch] * N_BRANCH))


def _head_norm_rope(t, g, cos, sin_lo, sin_hi):
    t = t * lax.rsqrt(jnp.mean(t * t, axis=-1, keepdims=True) + NORM_EPS) * g
    return t * cos + pltpu.roll(t, 96, 1) * sin_lo + pltpu.roll(t, 32, 1) * sin_hi


def _kprep_body(k_ref, g_ref, cos_ref, lo_ref, hi_ref, o_ref):
    for hk in range(N_KV_HEADS):
        sl = slice(hk * HEAD_DIM, (hk + 1) * HEAD_DIM)
        t = _head_norm_rope(k_ref[:, sl].astype(F32), g_ref[...], cos_ref[...], lo_ref[...], hi_ref[...])
        o_ref[:, sl] = t.astype(o_ref.dtype)


def _kprep(p, k_g, cos, sin_lo, sin_hi):
    tab = pl.BlockSpec((BR, HEAD_DIM), lambda i: (i, 0))
    return pl.pallas_call(
        _kprep_body,
        grid=(S_ALL // BR,),
        in_specs=[pl.BlockSpec((BR, KV_W), lambda i: (i, COL_K // KV_W)),
                  pl.BlockSpec((1, HEAD_DIM), lambda i: (0, 0)), tab, tab, tab],
        out_specs=pl.BlockSpec((BR, KV_W), lambda i: (i, 0)),
        out_shape=jax.ShapeDtypeStruct((S_ALL, KV_W), BF16),
        compiler_params=_params(("arbitrary",), 32),
        name="kprep",
    )(p, k_g, cos, sin_lo, sin_hi)


def _attn_body(sink_ref, q_ref, g_ref, cos_ref, lo_ref, hi_ref,
               kp_ref, kc_ref, kn_ref, kx_ref, vp_ref, vc_ref, vn_ref, vx_ref, o_ref):
    n = pl.program_id(0)
    nb_lat = SEQ // ATTN_BLOCK
    rows = GQA_GROUP * ATTN_BLOCK
    n_loc = 3 * ATTN_BLOCK
    is_lat = n < nb_lat
    lo = jnp.where(is_lat, jnp.where(n >= 1, 0, ATTN_BLOCK), n_loc)
    hi = jnp.where(n < nb_lat - 1, n_loc, 2 * ATTN_BLOCK)
    r = lax.broadcasted_iota(jnp.int32, (rows, n_loc), 0) & (ATTN_BLOCK - 1)
    a = lax.broadcasted_iota(jnp.int32, (rows, n_loc), 1)
    valid = (a >= r) & (a <= r + 2 * WINDOW) & (a >= lo) & (a < hi)
    row_head = lax.broadcasted_iota(jnp.int32, (rows, 1), 0) // ATTN_BLOCK
    scale = HEAD_DIM ** -0.5
    for hk in range(N_KV_HEADS):
        sl = slice(hk * HEAD_DIM, (hk + 1) * HEAD_DIM)
        qs = []
        for g in range(GQA_GROUP):
            h = hk * GQA_GROUP + g
            t = q_ref[:, h * HEAD_DIM:(h + 1) * HEAD_DIM].astype(F32)
            qs.append(_head_norm_rope(t, g_ref[...], cos_ref[...], lo_ref[...], hi_ref[...]).astype(BF16))
        q = jnp.concatenate(qs, axis=0)
        k_loc = jnp.concatenate([kp_ref[:, sl], kc_ref[:, sl], kn_ref[:, sl]], axis=0)
        v_loc = jnp.concatenate([vp_ref[:, sl], vc_ref[:, sl], vn_ref[:, sl]], axis=0)
        s_loc = jnp.where(valid, _dot_nt(q, k_loc) * scale, MASK_VALUE)
        s_ctx = _dot_nt(q, kx_ref[:, sl]) * scale
        snk = jnp.zeros((rows, 1), F32)
        for g in range(GQA_GROUP):
            snk = jnp.where(row_head == g, sink_ref[hk * GQA_GROUP + g], snk)
        m = jnp.maximum(jnp.maximum(jnp.max(s_loc, axis=-1, keepdims=True),
                                    jnp.max(s_ctx, axis=-1, keepdims=True)), snk)
        p_loc = jnp.exp(s_loc - m)
        p_ctx = jnp.exp(s_ctx - m)
        den = (jnp.sum(p_loc, axis=-1, keepdims=True) + jnp.sum(p_ctx, axis=-1, keepdims=True)
               + jnp.exp(snk - m))
        o = (_dot(p_loc.astype(BF16), v_loc) + _dot(p_ctx.astype(BF16), vx_ref[:, sl])) / den
        for g in range(GQA_GROUP):
            h = hk * GQA_GROUP + g
            o_ref[:, h * HEAD_DIM:(h + 1) * HEAD_DIM] = o[g * ATTN_BLOCK:(g + 1) * ATTN_BLOCK].astype(o_ref.dtype)


def _attention(p, k_rot, q_g, sink, cos, sin_lo, sin_hi):
    nb = S_ALL // ATTN_BLOCK
    blk_ctx = SEQ // CTX_LEN
    tab = pl.BlockSpec((ATTN_BLOCK, HEAD_DIM), lambda n: (n, 0))
    prev = lambda n: jnp.maximum(n - 1, 0)
    nxt = lambda n: jnp.minimum(n + 1, nb - 1)
    jv = COL_V // KV_W
    return pl.pallas_call(
        _attn_body,
        grid=(nb,),
        in_specs=[pl.BlockSpec(memory_space=pltpu.SMEM),
                  pl.BlockSpec((ATTN_BLOCK, Q_W), lambda n: (n, 0)),
                  pl.BlockSpec((1, HEAD_DIM), lambda n: (0, 0)), tab, tab, tab,
                  pl.BlockSpec((ATTN_BLOCK, KV_W), lambda n: (prev(n), 0)),
                  pl.BlockSpec((ATTN_BLOCK, KV_W), lambda n: (n, 0)),
                  pl.BlockSpec((ATTN_BLOCK, KV_W), lambda n: (nxt(n), 0)),
                  pl.BlockSpec((CTX_LEN, KV_W), lambda n: (blk_ctx, 0)),
                  pl.BlockSpec((ATTN_BLOCK, KV_W), lambda n: (prev(n), jv)),
                  pl.BlockSpec((ATTN_BLOCK, KV_W), lambda n: (n, jv)),
                  pl.BlockSpec((ATTN_BLOCK, KV_W), lambda n: (nxt(n), jv)),
                  pl.BlockSpec((CTX_LEN, KV_W), lambda n: (blk_ctx, jv))],
        out_specs=pl.BlockSpec((ATTN_BLOCK, Q_W), lambda n: (n, 0)),
        out_shape=jax.ShapeDtypeStruct((S_ALL, Q_W), BF16),
        compiler_params=_params(("arbitrary",), 32),
        name="attention",
    )(sink, p, q_g, cos, sin_lo, sin_hi, k_rot, k_rot, k_rot, k_rot, p, p, p, p)


def _sgu_body(u_ref, v_ref, lg_ref, lb_ref, ws_ref, bs_ref, o_ref):
    half = u_ref.shape[1] // SGU_CH
    for g in range(half):
        sl = slice(g * SGU_CH, (g + 1) * SGU_CH)
        vg = jax.nn.gelu(v_ref[:, sl].astype(F32))
        mu = jnp.mean(vg, axis=-1, keepdims=True)
        d = vg - mu
        var = jnp.mean(d * d, axis=-1, keepdims=True)
        vn = (d * lax.rsqrt(var + NORM_EPS) * lg_ref[:, sl] + lb_ref[:, sl]).astype(BF16)
        for c in range(BR // CHUNK):
            rs = slice(c * CHUNK, (c + 1) * CHUNK)
            mixed = _dot(ws_ref[g], vn[rs]) + bs_ref[:, g:g + 1]
            o_ref[rs, sl] = (jax.nn.gelu(u_ref[rs, sl].astype(F32)) * mixed).astype(o_ref.dtype)


def _sgu(p, ln_g, ln_b, w_s, b_s_t):
    bw = 512
    gh = bw // SGU_CH
    ju, jv = COL_SGU_U // bw, COL_SGU_V // bw
    return pl.pallas_call(
        _sgu_body,
        grid=(S_ALL // BR, SGU_W // bw),
        in_specs=[pl.BlockSpec((BR, bw), lambda i, c: (i, ju + c)),
                  pl.BlockSpec((BR, bw), lambda i, c: (i, jv + c)),
                  pl.BlockSpec((1, bw), lambda i, c: (0, c)),
                  pl.BlockSpec((1, bw), lambda i, c: (0, c)),
                  pl.BlockSpec((gh, CHUNK, CHUNK), lambda i, c: (c, 0, 0)),
                  pl.BlockSpec((None, CHUNK, gh), lambda i, c: (c, 0, 0))],
        out_specs=pl.BlockSpec((BR, bw), lambda i, c: (i, c)),
        out_shape=jax.ShapeDtypeStruct((S_ALL, SGU_W), BF16),
        compiler_params=_params(("arbitrary", "arbitrary"), 32),
        name="sgu",
    )(p, p, ln_g, ln_b, w_s, b_s_t)


HALO = 16


def _conv_body(x_ref, b_ref, c_ref, xp_ref, cp_ref, xn_ref, cn_ref, w_ref, o_ref):
    i = pl.program_id(0)
    first_ctx = SEQ // BR
    has_prev = (i != 0) & (i != first_ctx)
    has_next = (i != first_ctx - 1) & (i != S_ALL // BR - 1)
    z = c_ref[...].astype(F32) * x_ref[...].astype(F32)
    zp = cp_ref[HALO - 1:HALO, :].astype(F32) * xp_ref[HALO - 1:HALO, :].astype(F32)
    zn = cn_ref[0:1, :].astype(F32) * xn_ref[0:1, :].astype(F32)
    zp = jnp.where(has_prev, zp, 0.0)
    zn = jnp.where(has_next, zn, 0.0)
    row = lax.broadcasted_iota(jnp.int32, (BR, 1), 0)
    z_prev = jnp.where(row == 0, zp, pltpu.roll(z, 1, 0))
    z_next = jnp.where(row == BR - 1, zn, pltpu.roll(z, BR - 1, 0))
    y = w_ref[0:1, :] * z_prev + w_ref[1:2, :] * z + w_ref[2:3, :] * z_next
    o_ref[...] = (b_ref[...].astype(F32) * y).astype(o_ref.dtype)


def _short_conv(p, conv_w):
    bw = 512
    jx, jb, jc = COL_CONV_X // bw, COL_CONV_B // bw, COL_CONV_C // bw
    rb = BR // HALO
    nh = S_ALL // HALO
    prev = lambda i: jnp.maximum(i * rb - 1, 0)
    nxt = lambda i: jnp.minimum((i + 1) * rb, nh - 1)
    return pl.pallas_call(
        _conv_body,
        grid=(S_ALL // BR, CONV_W // bw),
        in_specs=[pl.BlockSpec((BR, bw), lambda i, c: (i, jx + c)),
                  pl.BlockSpec((BR, bw), lambda i, c: (i, jb + c)),
                  pl.BlockSpec((BR, bw), lambda i, c: (i, jc + c)),
                  pl.BlockSpec((HALO, bw), lambda i, c: (prev(i), jx + c)),
                  pl.BlockSpec((HALO, bw), lambda i, c: (prev(i), jc + c)),
                  pl.BlockSpec((HALO, bw), lambda i, c: (nxt(i), jx + c)),
                  pl.BlockSpec((HALO, bw), lambda i, c: (nxt(i), jc + c)),
                  pl.BlockSpec((3, bw), lambda i, c: (0, c))],
        out_specs=pl.BlockSpec((BR, bw), lambda i, c: (i, c)),
        out_shape=jax.ShapeDtypeStruct((S_ALL, CONV_W), BF16),
        compiler_params=_params(("arbitrary", "arbitrary"), 32),
        name="short_conv",
    )(p, p, p, p, p, p, p, conv_w)


def _dft_consts():
    def cs(n, rows, cols):
        ang = 2.0 * np.pi * ((np.outer(rows, cols)) % n) / n
        return np.cos(ang), np.sin(ang)

    t1, t2 = DFT_T1, DFT_T2
    c1, s1 = cs(t1, np.arange(t1), np.arange(t1))
    m1 = np.concatenate([c1, -s1], axis=0)
    cw, sw = cs(SEQ, np.arange(t1), np.arange(t2))
    c2, s2 = cs(t2, np.arange(t2), np.arange(t2))
    m2 = np.block([[c2, s2], [-s2, c2]])
    cx, sx = cs(CTX_LEN, np.arange(CTX_LEN), np.arange(CTX_LEN))
    m2x = np.block([[cx, sx], [-sx, cx]])
    cc, sc = cs(FOURIER_CH, np.arange(FOURIER_CH), np.arange(FOURIER_CH))
    nl = 1.0 / np.sqrt(SEQ * FOURIER_CH)
    nx = 1.0 / np.sqrt(CTX_LEN * FOURIER_CH)
    as_bf = lambda a: jnp.asarray(a, dtype=F32).astype(BF16)
    return dict(
        m1=as_bf(m1), m2=as_bf(m2), m2x=as_bf(m2x),
        cw=jnp.asarray(cw[:, :, None], F32), sw=jnp.asarray(sw[:, :, None], F32),
        cwx=jnp.ones((1, CTX_LEN, 1), F32), swx=jnp.zeros((1, CTX_LEN, 1), F32),
        cc_l=as_bf(cc * nl), sc_l=as_bf(sc * nl), cc_x=as_bf(cc * nx), sc_x=as_bf(sc * nx))


def _dft1_body(m_ref, x_ref, o_ref):
    o_ref[...] = _dot(m_ref[...], x_ref[...]).astype(o_ref.dtype)


def _dft_stage1(f, m1):
    ncol = DFT_T2 * FOURIER_W
    bn = 8192
    fv = f.reshape(S_ALL // DFT_T2, ncol)
    return pl.pallas_call(
        _dft1_body,
        grid=(ncol // bn,),
        in_specs=[pl.BlockSpec((2 * DFT_T1, DFT_T1), lambda j: (0, 0)),
                  pl.BlockSpec((DFT_T1, bn), lambda j: (0, j))],
        out_specs=pl.BlockSpec((2 * DFT_T1, bn), lambda j: (0, j)),
        out_shape=jax.ShapeDtypeStruct((2 * DFT_T1, ncol), BF16),
        compiler_params=_params(("arbitrary",), 32),
        name="dft_stage1",
    )(m1, fv)


def _dft2_body(ar_ref, ai_ref, cw_ref, sw_ref, m2_ref, cc_ref, sc_ref, o_ref):
    t2 = ar_ref.shape[0]
    ar = ar_ref[...].astype(F32)
    ai = ai_ref[...].astype(F32)
    cw = cw_ref[...]
    sw = sw_ref[...]
    b = jnp.concatenate([ar * cw + ai * sw, ai * cw - ar * sw], axis=0).astype(BF16)
    p = _dot(m2_ref[...], b)
    pr = p[:t2].astype(BF16)
    pi = p[t2:].astype(BF16)
    for g in range(FOURIER_GROUPS):
        sl = slice(g * FOURIER_CH, (g + 1) * FOURIER_CH)
        o_ref[:, sl] = (_dot(pr[:, sl], cc_ref[...]) + _dot(pi[:, sl], sc_ref[...])).astype(o_ref.dtype)


def _dft_stage2(a3, n_ka, t2, im_off, cw, sw, m2, cc, sc):
    return pl.pallas_call(
        _dft2_body,
        grid=(n_ka,),
        in_specs=[pl.BlockSpec((None, t2, FOURIER_W), lambda ka: (ka, 0, 0)),
                  pl.BlockSpec((None, t2, FOURIER_W), lambda ka: (im_off + ka, 0, 0)),
                  pl.BlockSpec((None, t2, 1), lambda ka: (ka, 0, 0)),
                  pl.BlockSpec((None, t2, 1), lambda ka: (ka, 0, 0)),
                  pl.BlockSpec((2 * t2, 2 * t2), lambda ka: (0, 0)),
                  pl.BlockSpec((FOURIER_CH, FOURIER_CH), lambda ka: (0, 0)),
                  pl.BlockSpec((FOURIER_CH, FOURIER_CH), lambda ka: (0, 0))],
        out_specs=pl.BlockSpec((t2, FOURIER_W), lambda ka: (0, ka)),
        out_shape=jax.ShapeDtypeStruct((t2, n_ka * FOURIER_W), BF16),
        compiler_params=_params(("arbitrary",), 32),
        name="dft_stage2",
    )(a3, a3, cw, sw, m2, cc, sc)


def _fourier(f, k):
    a = _dft_stage1(f, k["m1"]).reshape(2 * DFT_T1, DFT_T2, FOURIER_W)
    y_lat = _dft_stage2(a, DFT_T1, DFT_T2, DFT_T1, k["cw"], k["sw"], k["m2"], k["cc_l"], k["sc_l"])
    y_lat = y_lat.reshape(SEQ, FOURIER_W)
    fx = jnp.stack([f[SEQ:], jnp.zeros((CTX_LEN, FOURIER_W), f.dtype)])
    y_ctx = _dft_stage2(fx, 1, CTX_LEN, 1, k["cwx"], k["swx"], k["m2x"], k["cc_x"], k["sc_x"])
    return jnp.concatenate([y_lat, y_ctx], axis=0)


def _rope_tables():
    pos = np.arange(SEQ)
    ax = HEAD_DIM // 2
    inv_freq = 1.0 / (ROPE_THETA ** (np.arange(0, ax, 2, dtype=np.float32) / ax)).astype(np.float32)
    ang_r = (pos // GRID_W).astype(np.float32)[:, None] * inv_freq[None, :]
    ang_c = (pos % GRID_W).astype(np.float32)[:, None] * inv_freq[None, :]
    emb = np.concatenate([ang_r, ang_r, ang_c, ang_c], axis=-1).astype(np.float32)
    cos = np.cos(emb.astype(np.float64))
    sin = np.sin(emb.astype(np.float64))
    first = (np.arange(HEAD_DIM) % (HEAD_DIM // 2)) < HEAD_DIM // 4
    sin_lo = np.where(first[None, :], -sin, 0.0)
    sin_hi = np.where(first[None, :], 0.0, sin)
    pad = lambda a, v: np.concatenate([a, np.full((CTX_LEN, HEAD_DIM), v)], axis=0)
    return (jnp.asarray(pad(cos, 1.0), F32), jnp.asarray(pad(sin_lo, 0.0), F32),
            jnp.asarray(pad(sin_hi, 0.0), F32))


def kernel(x, c, ctx, c_ctx, ada_down, ada_up, ada_b, norm_g, ffn_wi, ffn_wo, w_in, q_norm, k_norm,
           sink, sgu_ln_g, sgu_ln_b, sgu_w, sgu_b, conv_w, w_branch, w_out):
    assert x.shape == (1, SEQ, D_MODEL) and ctx.shape == (1, CTX_LEN, D_MODEL)
    h = jnp.concatenate([x[0], ctx[0]], axis=0)
    cv = jnp.concatenate([c, c_ctx[None, :], jnp.zeros((6, D_MODEL), F32)], axis=0)
    m_all = _ada_mod(cv, ada_down, ada_up, ada_b)
    cos, sin_lo, sin_hi = _rope_tables()
    dft = _dft_consts()

    def ffn(h, m, g_row, wi, wo, i_mod):
        z = _norm_mod(h, g_row, m, i_mod)
        return _mm_resid(_mm_swiglu(z, wi.astype(BF16)), wo.astype(BF16), h, m, i_mod, 0.5)

    for l in range(DEPTH):
        m = m_all[l]
        h = ffn(h, m, norm_g[l, 0][None, :], ffn_wi[l, 0], ffn_wo[l, 0], 0)
        z = _norm_mod(h, norm_g[l, 1][None, :], m, 1)
        w_in_l = w_in[l].astype(BF16)
        p = _mm_cols(z, w_in_l, 0, COL_FOURIER)
        f = _mm_cols(z, w_in_l, COL_FOURIER, FOURIER_W)
        k_rot = _kprep(p, k_norm[l][None, :], cos, sin_lo, sin_hi)
        ys = (_attention(p, k_rot, q_norm[l][None, :], sink[l], cos, sin_lo, sin_hi),
              _sgu(p, sgu_ln_g[l][None, :], sgu_ln_b[l][None, :], sgu_w[l].astype(BF16),
                   sgu_b[l].reshape(SGU_W // 512, 512 // SGU_CH, CHUNK).transpose(0, 2, 1)),
              _short_conv(p, conv_w[l]),
              _fourier(f, dft))
        merged = _merge(z, ys, w_in_l, w_branch[l].astype(BF16).reshape(N_BRANCH * BRANCH_W, D_MODEL))
        h = _mm_resid(merged, w_out[l].astype(BF16), h, m, 1, 1.0)
        h = ffn(h, m, norm_g[l, 2][None, :], ffn_wi[l, 1], ffn_wo[l, 1], 2)
    return h[:SEQ][None]
```

```python
import functools
from typing import NamedTuple

import numpy as np
import jax
import jax.numpy as jnp
from jax import lax
from jax.experimental import pallas as pl
from jax.experimental.pallas import tpu as pltpu

D_MODEL = 4096
SEQ = 8192
DEPTH = 4
GRID_W = 64
CTX_LEN = 256
S_ALL = SEQ + CTX_LEN
HEAD_DIM = 128
N_Q_HEADS = 8
N_KV_HEADS = 2
GQA_GROUP = N_Q_HEADS // N_KV_HEADS
Q_W = N_Q_HEADS * HEAD_DIM
KV_W = N_KV_HEADS * HEAD_DIM
WINDOW = 128
ATTN_BLOCK = 128
ROPE_THETA = 10000.0
SGU_GROUPS = 8
SGU_CH = 128
SGU_W = SGU_GROUPS * SGU_CH
CHUNK = 128
CONV_W = 1024
FOURIER_GROUPS = 4
FOURIER_CH = 256
FOURIER_W = FOURIER_GROUPS * FOURIER_CH
N_BRANCH = 4
BRANCH_W = 1024
D_FF = 3584
ADA_RANK = 256
N_MOD = 9
NORM_EPS = 1e-6
MASK_VALUE = -1e30

COL_Q = 0
COL_K = Q_W
COL_V = COL_K + KV_W
COL_SGU_U = COL_V + KV_W
COL_SGU_V = COL_SGU_U + SGU_W
COL_CONV_X = COL_SGU_V + SGU_W
COL_CONV_B = COL_CONV_X + CONV_W
COL_CONV_C = COL_CONV_B + CONV_W
COL_FOURIER = COL_CONV_C + CONV_W
COL_GATES = COL_FOURIER + FOURIER_W
IN_COLS = COL_GATES + N_BRANCH * D_MODEL

BM = 768
BM_RESID = 1408
BR = 256
DFT_T1 = 64
DFT_T2 = 128
MXU_N = 256
MIB = 1024 * 1024

F32 = jnp.float32
BF16 = jnp.bfloat16


def _params(sem, vmem_mib):
    return pltpu.CompilerParams(dimension_semantics=sem, vmem_limit_bytes=vmem_mib * MIB)


def _dot(a, b):
    return jnp.dot(a, b, preferred_element_type=F32)


def _dot_nt(a, b):
    return lax.dot_general(a, b, (((1,), (1,)), ((), ())), preferred_element_type=F32)


class _CastJob(NamedTuple):
    src: jax.Array
    lead: tuple
    rows: int
    col0: int
    ncols: int
    rb: int
    cb: int

    @property
    def nblk(self):
        return (self.rows // self.rb) * (self.ncols // self.cb)


def _cast_specs(job, nj):
    nbc = job.ncols // job.cb
    c0 = job.col0 // job.cb
    last = job.nblk - 1

    def src_map(i, j):
        tb = jnp.minimum(i * nj + j, last)
        return job.lead + (tb // nbc, c0 + tb % nbc)

    def dst_map(i, j):
        tb = jnp.minimum(i * nj + j, last)
        return (tb // nbc, tb % nbc)

    return (pl.BlockSpec((None,) * len(job.lead) + (job.rb, job.cb), src_map),
            pl.BlockSpec((job.rb, job.cb), dst_map),
            jax.ShapeDtypeStruct((job.rows, job.ncols), BF16))


def _host_call(body, grid, in_specs, out_spec, out_shape, args, jobs, vmem_mib, name):
    n_in, n_jobs = len(in_specs), len(jobs)
    assert all(j.nblk <= grid[0] * grid[1] for j in jobs)
    specs = [_cast_specs(j, grid[1]) for j in jobs]

    def wrapped(*refs):
        body(*refs[:n_in], refs[n_in + n_jobs])
        for s_ref, d_ref in zip(refs[n_in:n_in + n_jobs], refs[n_in + n_jobs + 1:]):
            d_ref[...] = s_ref[...].astype(BF16)

    outs = pl.pallas_call(
        wrapped,
        grid=grid,
        in_specs=list(in_specs) + [s[0] for s in specs],
        out_specs=[out_spec] + [s[1] for s in specs],
        out_shape=[out_shape] + [s[2] for s in specs],
        compiler_params=_params(("arbitrary", "arbitrary"), vmem_mib),
        name=name,
    )(*args, *[j.src for j in jobs])
    return outs[0], list(outs[1:])


def _cast_body(s_ref, d_ref):
    d_ref[...] = s_ref[...].astype(BF16)


def _cast_now(job):
    src_spec, dst_spec, shape = _cast_specs(job, 1)
    return pl.pallas_call(
        _cast_body,
        grid=(job.nblk, 1),
        in_specs=[src_spec],
        out_specs=dst_spec,
        out_shape=shape,
        compiler_params=_params(("arbitrary", "arbitrary"), 32),
        name="cast_bf16",
    )(job.src)


def _ada_down_body(cv_ref, down_ref, o_ref):
    cv = cv_ref[...]
    s = (cv * jax.nn.sigmoid(cv)).astype(BF16)
    o_ref[...] = _dot(s, down_ref[...].astype(BF16))


def _ada_up_body(t_ref, up_ref, b_ref, o_ref):
    o_ref[...] = _dot(t_ref[...].astype(BF16), up_ref[...].astype(BF16)) + b_ref[...]


def _ada_mod(cv, ada_down, ada_up, ada_b):
    t = pl.pallas_call(
        _ada_down_body,
        grid=(DEPTH,),
        in_specs=[pl.BlockSpec((8, D_MODEL), lambda l: (0, 0)),
                  pl.BlockSpec((None, D_MODEL, ADA_RANK), lambda l: (l, 0, 0))],
        out_specs=pl.BlockSpec((None, 8, ADA_RANK), lambda l: (l, 0, 0)),
        out_shape=jax.ShapeDtypeStruct((DEPTH, 8, ADA_RANK), F32),
        compiler_params=_params(("arbitrary",), 32),
        name="ada_down",
    )(cv, ada_down)
    bn = 4608
    nj = N_MOD * D_MODEL // bn
    return pl.pallas_call(
        _ada_up_body,
        grid=(DEPTH, nj),
        in_specs=[pl.BlockSpec((None, 8, ADA_RANK), lambda l, j: (l, 0, 0)),
                  pl.BlockSpec((None, ADA_RANK, bn), lambda l, j: (l, 0, j)),
                  pl.BlockSpec((None, 1, bn), lambda l, j: (l, 0, j))],
        out_specs=pl.BlockSpec((None, 8, bn), lambda l, j: (l, 0, j)),
        out_shape=jax.ShapeDtypeStruct((DEPTH, 8, N_MOD * D_MODEL), F32),
        compiler_params=_params(("arbitrary", "arbitrary"), 32),
        name="ada_up",
    )(t, ada_up, ada_b.reshape(DEPTH, 1, N_MOD * D_MODEL))


def _norm_mod_body(h_ref, g_ref, shift_ref, scale_ref, o_ref):
    is_ctx = pl.program_id(0) >= SEQ // BR
    x = h_ref[...]
    r = lax.rsqrt(jnp.mean(x * x, axis=-1, keepdims=True) + NORM_EPS)
    scale = jnp.where(is_ctx, scale_ref[1:2, :], scale_ref[0:1, :])
    shift = jnp.where(is_ctx, shift_ref[1:2, :], shift_ref[0:1, :])
    y = (x * r) * g_ref[...]
    o_ref[...] = (y * (1.0 + scale) + shift).astype(o_ref.dtype)


def _norm_mod(h, g_row, m, i_mod):
    return pl.pallas_call(
        _norm_mod_body,
        grid=(S_ALL // BR,),
        in_specs=[pl.BlockSpec((BR, D_MODEL), lambda i: (i, 0)),
                  pl.BlockSpec((1, D_MODEL), lambda i: (0, 0)),
                  pl.BlockSpec((8, D_MODEL), lambda i: (0, 3 * i_mod)),
                  pl.BlockSpec((8, D_MODEL), lambda i: (0, 3 * i_mod + 1))],
        out_specs=pl.BlockSpec((BR, D_MODEL), lambda i: (i, 0)),
        out_shape=jax.ShapeDtypeStruct((S_ALL, D_MODEL), BF16),
        compiler_params=_params(("arbitrary",), 32),
        name="norm_mod",
    )(h, g_row, m, m)


def _col_chunks(ref):
    return [slice(c, c + MXU_N) for c in range(0, ref.shape[1], MXU_N)]


def _swiglu_body(z_ref, wa_ref, wb_ref, o_ref):
    z = z_ref[...]
    for sl in _col_chunks(o_ref):
        a = _dot(z, wa_ref[:, sl])
        b = _dot(z, wb_ref[:, sl])
        o_ref[:, sl] = (a * jax.nn.sigmoid(a) * b).astype(o_ref.dtype)


def _mm_swiglu(z, wi, jobs=()):
    bn = 512
    nj = D_FF // bn
    return _host_call(
        _swiglu_body, (S_ALL // BM, nj),
        [pl.BlockSpec((BM, D_MODEL), lambda i, j: (i, 0)),
         pl.BlockSpec((D_MODEL, bn), lambda i, j: (0, j)),
         pl.BlockSpec((D_MODEL, bn), lambda i, j: (0, j + nj))],
        pl.BlockSpec((BM, bn), lambda i, j: (i, j)),
        jax.ShapeDtypeStruct((S_ALL, D_FF), BF16),
        (z, wi, wi), jobs, 56, "mm_swiglu")


def _resid_body(x_ref, w_ref, h_ref, gate_ref, o_ref, *, coef):
    x = x_ref[...]
    bm = x_ref.shape[0]
    is_ctx = pl.program_id(0) * bm + lax.broadcasted_iota(jnp.int32, (bm, 1), 0) >= SEQ
    for sl in _col_chunks(o_ref):
        gate = jnp.where(is_ctx, gate_ref[1:2, sl], gate_ref[0:1, sl])
        o_ref[:, sl] = h_ref[:, sl] + (coef * gate) * _dot(x, w_ref[:, sl])


def _mm_resid(xin, w, h, m, i_mod, coef, jobs=()):
    k = xin.shape[1]
    bm, bn = BM_RESID, 512
    nj = D_MODEL // bn
    return _host_call(
        functools.partial(_resid_body, coef=coef), (S_ALL // bm, nj),
        [pl.BlockSpec((bm, k), lambda i, j: (i, 0)),
         pl.BlockSpec((k, bn), lambda i, j: (0, j)),
         pl.BlockSpec((bm, bn), lambda i, j: (i, j)),
         pl.BlockSpec((8, bn), lambda i, j: (0, (3 * i_mod + 2) * nj + j))],
        pl.BlockSpec((bm, bn), lambda i, j: (i, j)),
        jax.ShapeDtypeStruct((S_ALL, D_MODEL), F32),
        (xin, w, h, m), jobs, 56, "mm_resid")


def _plain_body(z_ref, w_ref, o_ref):
    z = z_ref[...]
    for sl in _col_chunks(o_ref):
        o_ref[:, sl] = _dot(z, w_ref[:, sl]).astype(o_ref.dtype)


def _mm_cols(z, w, col0, ncols, jobs=()):
    bn = 512
    j0 = col0 // bn
    return _host_call(
        _plain_body, (S_ALL // BM, ncols // bn),
        [pl.BlockSpec((BM, D_MODEL), lambda i, j: (i, 0)),
         pl.BlockSpec((D_MODEL, bn), lambda i, j: (0, j0 + j))],
        pl.BlockSpec((BM, bn), lambda i, j: (i, j)),
        jax.ShapeDtypeStruct((S_ALL, ncols), BF16),
        (z, w), jobs, 48, "mm_cols")


def _merge_body(z_ref, y0, y1, y2, y3, g0, g1, g2, g3, b0, b1, b2, b3, o_ref):
    z = z_ref[...]
    acc = None
    for y_ref, g_ref, b_ref in ((y0, g0, b0), (y1, g1, b1), (y2, g2, b2), (y3, g3, b3)):
        t = jax.nn.sigmoid(_dot(z, g_ref[...])) * _dot(y_ref[...], b_ref[...])
        acc = t if acc is None else acc + t
    o_ref[...] = acc.astype(o_ref.dtype)


def _merge(z, ys, w_gates, w_branch, jobs=()):
    bn = 256
    nj = D_MODEL // bn
    z_spec = pl.BlockSpec((BM, D_MODEL), lambda i, j: (i, 0))
    y_spec = pl.BlockSpec((BM, BRANCH_W), lambda i, j: (i, 0), pipeline_mode=pl.Buffered(1))
    g_specs = [pl.BlockSpec((D_MODEL, bn), functools.partial(lambda i, j, r: (0, r * nj + j), r=r))
               for r in range(N_BRANCH)]
    b_specs = [pl.BlockSpec((BRANCH_W, bn), functools.partial(lambda i, j, r: (r, j), r=r))
               for r in range(N_BRANCH)]
    return _host_call(
        _merge_body, (S_ALL // BM, nj),
        [z_spec] + [y_spec] * N_BRANCH + g_specs + b_specs,
        pl.BlockSpec((BM, bn), lambda i, j: (i, j)),
        jax.ShapeDtypeStruct((S_ALL, D_MODEL), BF16),
        (z, *ys, *([w_gates] * N_BRANCH), *([w_branch] * N_BRANCH)), jobs, 56, "merge")


def _head_norm_rope(t, g, cos, sin_lo, sin_hi):
    t = t * lax.rsqrt(jnp.mean(t * t, axis=-1, keepdims=True) + NORM_EPS) * g
    return t * cos + pltpu.roll(t, 96, 1) * sin_lo + pltpu.roll(t, 32, 1) * sin_hi


def _kprep_body(k_ref, g_ref, cos_ref, lo_ref, hi_ref, o_ref):
    for hk in range(N_KV_HEADS):
        sl = slice(hk * HEAD_DIM, (hk + 1) * HEAD_DIM)
        t = _head_norm_rope(k_ref[:, sl].astype(F32), g_ref[...], cos_ref[...], lo_ref[...], hi_ref[...])
        o_ref[:, sl] = t.astype(o_ref.dtype)


def _kprep(p, k_g, cos, sin_lo, sin_hi):
    tab = pl.BlockSpec((BR, HEAD_DIM), lambda i: (i, 0))
    return pl.pallas_call(
        _kprep_body,
        grid=(S_ALL // BR,),
        in_specs=[pl.BlockSpec((BR, KV_W), lambda i: (i, COL_K // KV_W)),
                  pl.BlockSpec((1, HEAD_DIM), lambda i: (0, 0)), tab, tab, tab],
        out_specs=pl.BlockSpec((BR, KV_W), lambda i: (i, 0)),
        out_shape=jax.ShapeDtypeStruct((S_ALL, KV_W), BF16),
        compiler_params=_params(("arbitrary",), 32),
        name="kprep",
    )(p, k_g, cos, sin_lo, sin_hi)


def _attn_body(sink_ref, q_ref, g_ref, cos_ref, lo_ref, hi_ref,
               kp_ref, kc_ref, kn_ref, kx_ref, vp_ref, vc_ref, vn_ref, vx_ref, o_ref):
    n = pl.program_id(0)
    nb_lat = SEQ // ATTN_BLOCK
    rows = GQA_GROUP * ATTN_BLOCK
    n_loc = 3 * ATTN_BLOCK
    is_lat = n < nb_lat
    lo = jnp.where(is_lat, jnp.where(n >= 1, 0, ATTN_BLOCK), n_loc)
    hi = jnp.where(n < nb_lat - 1, n_loc, 2 * ATTN_BLOCK)
    r = lax.broadcasted_iota(jnp.int32, (rows, n_loc), 0) & (ATTN_BLOCK - 1)
    a = lax.broadcasted_iota(jnp.int32, (rows, n_loc), 1)
    valid = (a >= r) & (a <= r + 2 * WINDOW) & (a >= lo) & (a < hi)
    row_head = lax.broadcasted_iota(jnp.int32, (rows, 1), 0) // ATTN_BLOCK
    scale = HEAD_DIM ** -0.5
    for hk in range(N_KV_HEADS):
        sl = slice(hk * HEAD_DIM, (hk + 1) * HEAD_DIM)
        qs = []
        for g in range(GQA_GROUP):
            h = hk * GQA_GROUP + g
            t = q_ref[:, h * HEAD_DIM:(h + 1) * HEAD_DIM].astype(F32)
            qs.append(_head_norm_rope(t, g_ref[...], cos_ref[...], lo_ref[...], hi_ref[...]).astype(BF16))
        q = jnp.concatenate(qs, axis=0)
        k_loc = jnp.concatenate([kp_ref[:, sl], kc_ref[:, sl], kn_ref[:, sl]], axis=0)
        v_loc = jnp.concatenate([vp_ref[:, sl], vc_ref[:, sl], vn_ref[:, sl]], axis=0)
        s_loc = jnp.where(valid, _dot_nt(q, k_loc) * scale, MASK_VALUE)
        s_ctx = _dot_nt(q, kx_ref[:, sl]) * scale
        snk = jnp.zeros((rows, 1), F32)
        for g in range(GQA_GROUP):
            snk = jnp.where(row_head == g, sink_ref[hk * GQA_GROUP + g], snk)
        m = jnp.maximum(jnp.maximum(jnp.max(s_loc, axis=-1, keepdims=True),
                                    jnp.max(s_ctx, axis=-1, keepdims=True)), snk)
        p_loc = jnp.exp(s_loc - m)
        p_ctx = jnp.exp(s_ctx - m)
        den = (jnp.sum(p_loc, axis=-1, keepdims=True) + jnp.sum(p_ctx, axis=-1, keepdims=True)
               + jnp.exp(snk - m))
        o = (_dot(p_loc.astype(BF16), v_loc) + _dot(p_ctx.astype(BF16), vx_ref[:, sl])) / den
        for g in range(GQA_GROUP):
            h = hk * GQA_GROUP + g
            o_ref[:, h * HEAD_DIM:(h + 1) * HEAD_DIM] = o[g * ATTN_BLOCK:(g + 1) * ATTN_BLOCK].astype(o_ref.dtype)


def _attention(p, k_rot, q_g, sink, cos, sin_lo, sin_hi):
    nb = S_ALL // ATTN_BLOCK
    blk_ctx = SEQ // CTX_LEN
    tab = pl.BlockSpec((ATTN_BLOCK, HEAD_DIM), lambda n: (n, 0))
    prev = lambda n: jnp.maximum(n - 1, 0)
    nxt = lambda n: jnp.minimum(n + 1, nb - 1)
    jv = COL_V // KV_W
    return pl.pallas_call(
        _attn_body,
        grid=(nb,),
        in_specs=[pl.BlockSpec(memory_space=pltpu.SMEM),
                  pl.BlockSpec((ATTN_BLOCK, Q_W), lambda n: (n, 0)),
                  pl.BlockSpec((1, HEAD_DIM), lambda n: (0, 0)), tab, tab, tab,
                  pl.BlockSpec((ATTN_BLOCK, KV_W), lambda n: (prev(n), 0)),
                  pl.BlockSpec((ATTN_BLOCK, KV_W), lambda n: (n, 0)),
                  pl.BlockSpec((ATTN_BLOCK, KV_W), lambda n: (nxt(n), 0)),
                  pl.BlockSpec((CTX_LEN, KV_W), lambda n: (blk_ctx, 0)),
                  pl.BlockSpec((ATTN_BLOCK, KV_W), lambda n: (prev(n), jv)),
                  pl.BlockSpec((ATTN_BLOCK, KV_W), lambda n: (n, jv)),
                  pl.BlockSpec((ATTN_BLOCK, KV_W), lambda n: (nxt(n), jv)),
                  pl.BlockSpec((CTX_LEN, KV_W), lambda n: (blk_ctx, jv))],
        out_specs=pl.BlockSpec((ATTN_BLOCK, Q_W), lambda n: (n, 0)),
        out_shape=jax.ShapeDtypeStruct((S_ALL, Q_W), BF16),
        compiler_params=_params(("arbitrary",), 32),
        name="attention",
    )(sink, p, q_g, cos, sin_lo, sin_hi, k_rot, k_rot, k_rot, k_rot, p, p, p, p)


def _sgu_body(u_ref, v_ref, lg_ref, lb_ref, ws_ref, bs_ref, o_ref):
    half = u_ref.shape[1] // SGU_CH
    for g in range(half):
        sl = slice(g * SGU_CH, (g + 1) * SGU_CH)
        vg = jax.nn.gelu(v_ref[:, sl].astype(F32))
        mu = jnp.mean(vg, axis=-1, keepdims=True)
        d = vg - mu
        var = jnp.mean(d * d, axis=-1, keepdims=True)
        vn = (d * lax.rsqrt(var + NORM_EPS) * lg_ref[:, sl] + lb_ref[:, sl]).astype(BF16)
        for c in range(BR // CHUNK):
            rs = slice(c * CHUNK, (c + 1) * CHUNK)
            mixed = _dot(ws_ref[g], vn[rs]) + bs_ref[:, g:g + 1]
            o_ref[rs, sl] = (jax.nn.gelu(u_ref[rs, sl].astype(F32)) * mixed).astype(o_ref.dtype)


def _sgu(p, ln_g, ln_b, w_s, b_s_t):
    bw = 512
    gh = bw // SGU_CH
    ju, jv = COL_SGU_U // bw, COL_SGU_V // bw
    return pl.pallas_call(
        _sgu_body,
        grid=(S_ALL // BR, SGU_W // bw),
        in_specs=[pl.BlockSpec((BR, bw), lambda i, c: (i, ju + c)),
                  pl.BlockSpec((BR, bw), lambda i, c: (i, jv + c)),
                  pl.BlockSpec((1, bw), lambda i, c: (0, c)),
                  pl.BlockSpec((1, bw), lambda i, c: (0, c)),
                  pl.BlockSpec((gh, CHUNK, CHUNK), lambda i, c: (c, 0, 0)),
                  pl.BlockSpec((None, CHUNK, gh), lambda i, c: (c, 0, 0))],
        out_specs=pl.BlockSpec((BR, bw), lambda i, c: (i, c)),
        out_shape=jax.ShapeDtypeStruct((S_ALL, SGU_W), BF16),
        compiler_params=_params(("arbitrary", "arbitrary"), 32),
        name="sgu",
    )(p, p, ln_g, ln_b, w_s, b_s_t)


HALO = 16


def _conv_body(x_ref, b_ref, c_ref, xp_ref, cp_ref, xn_ref, cn_ref, w_ref, o_ref):
    i = pl.program_id(0)
    first_ctx = SEQ // BR
    has_prev = (i != 0) & (i != first_ctx)
    has_next = (i != first_ctx - 1) & (i != S_ALL // BR - 1)
    z = c_ref[...].astype(F32) * x_ref[...].astype(F32)
    zp = cp_ref[HALO - 1:HALO, :].astype(F32) * xp_ref[HALO - 1:HALO, :].astype(F32)
    zn = cn_ref[0:1, :].astype(F32) * xn_ref[0:1, :].astype(F32)
    zp = jnp.where(has_prev, zp, 0.0)
    zn = jnp.where(has_next, zn, 0.0)
    row = lax.broadcasted_iota(jnp.int32, (BR, 1), 0)
    z_prev = jnp.where(row == 0, zp, pltpu.roll(z, 1, 0))
    z_next = jnp.where(row == BR - 1, zn, pltpu.roll(z, BR - 1, 0))
    y = w_ref[0:1, :] * z_prev + w_ref[1:2, :] * z + w_ref[2:3, :] * z_next
    o_ref[...] = (b_ref[...].astype(F32) * y).astype(o_ref.dtype)


def _short_conv(p, conv_w):
    bw = 512
    jx, jb, jc = COL_CONV_X // bw, COL_CONV_B // bw, COL_CONV_C // bw
    rb = BR // HALO
    nh = S_ALL // HALO
    prev = lambda i: jnp.maximum(i * rb - 1, 0)
    nxt = lambda i: jnp.minimum((i + 1) * rb, nh - 1)
    return pl.pallas_call(
        _conv_body,
        grid=(S_ALL // BR, CONV_W // bw),
        in_specs=[pl.BlockSpec((BR, bw), lambda i, c: (i, jx + c)),
                  pl.BlockSpec((BR, bw), lambda i, c: (i, jb + c)),
                  pl.BlockSpec((BR, bw), lambda i, c: (i, jc + c)),
                  pl.BlockSpec((HALO, bw), lambda i, c: (prev(i), jx + c)),
                  pl.BlockSpec((HALO, bw), lambda i, c: (prev(i), jc + c)),
                  pl.BlockSpec((HALO, bw), lambda i, c: (nxt(i), jx + c)),
                  pl.BlockSpec((HALO, bw), lambda i, c: (nxt(i), jc + c)),
                  pl.BlockSpec((3, bw), lambda i, c: (0, c))],
        out_specs=pl.BlockSpec((BR, bw), lambda i, c: (i, c)),
        out_shape=jax.ShapeDtypeStruct((S_ALL, CONV_W), BF16),
        compiler_params=_params(("arbitrary", "arbitrary"), 32),
        name="short_conv",
    )(p, p, p, p, p, p, p, conv_w)


def _dft_consts():
    def cs(n, rows, cols):
        ang = 2.0 * np.pi * ((np.outer(rows, cols)) % n) / n
        return np.cos(ang), np.sin(ang)

    t1, t2 = DFT_T1, DFT_T2
    c1, s1 = cs(t1, np.arange(t1), np.arange(t1))
    m1 = np.concatenate([c1, -s1], axis=0)
    cw, sw = cs(SEQ, np.arange(t1), np.arange(t2))
    c2, s2 = cs(t2, np.arange(t2), np.arange(t2))
    m2 = np.block([[c2, s2], [-s2, c2]])
    cx, sx = cs(CTX_LEN, np.arange(CTX_LEN), np.arange(CTX_LEN))
    m2x = np.block([[cx, sx], [-sx, cx]])
    cc, sc = cs(FOURIER_CH, np.arange(FOURIER_CH), np.arange(FOURIER_CH))
    nl = 1.0 / np.sqrt(SEQ * FOURIER_CH)
    nx = 1.0 / np.sqrt(CTX_LEN * FOURIER_CH)
    as_bf = lambda a: jnp.asarray(a, dtype=F32).astype(BF16)
    return dict(
        m1=as_bf(m1), m2=as_bf(m2), m2x=as_bf(m2x),
        cw=jnp.asarray(cw[:, :, None], F32), sw=jnp.asarray(sw[:, :, None], F32),
        cwx=jnp.ones((1, CTX_LEN, 1), F32), swx=jnp.zeros((1, CTX_LEN, 1), F32),
        cc_l=as_bf(cc * nl), sc_l=as_bf(sc * nl), cc_x=as_bf(cc * nx), sc_x=as_bf(sc * nx))


def _dft1_body(m_ref, x_ref, o_ref):
    o_ref[...] = _dot(m_ref[...], x_ref[...]).astype(o_ref.dtype)


def _dft_stage1(f, m1):
    ncol = DFT_T2 * FOURIER_W
    bn = 8192
    fv = f.reshape(S_ALL // DFT_T2, ncol)
    return pl.pallas_call(
        _dft1_body,
        grid=(ncol // bn,),
        in_specs=[pl.BlockSpec((2 * DFT_T1, DFT_T1), lambda j: (0, 0)),
                  pl.BlockSpec((DFT_T1, bn), lambda j: (0, j))],
        out_specs=pl.BlockSpec((2 * DFT_T1, bn), lambda j: (0, j)),
        out_shape=jax.ShapeDtypeStruct((2 * DFT_T1, ncol), BF16),
        compiler_params=_params(("arbitrary",), 32),
        name="dft_stage1",
    )(m1, fv)


def _dft2_body(ar_ref, ai_ref, cw_ref, sw_ref, m2_ref, cc_ref, sc_ref, o_ref):
    t2 = ar_ref.shape[0]
    ar = ar_ref[...].astype(F32)
    ai = ai_ref[...].astype(F32)
    cw = cw_ref[...]
    sw = sw_ref[...]
    b = jnp.concatenate([ar * cw + ai * sw, ai * cw - ar * sw], axis=0).astype(BF16)
    p = _dot(m2_ref[...], b)
    pr = p[:t2].astype(BF16)
    pi = p[t2:].astype(BF16)
    for g in range(FOURIER_GROUPS):
        sl = slice(g * FOURIER_CH, (g + 1) * FOURIER_CH)
        o_ref[:, sl] = (_dot(pr[:, sl], cc_ref[...]) + _dot(pi[:, sl], sc_ref[...])).astype(o_ref.dtype)


def _dft_stage2(a3, n_ka, t2, im_off, cw, sw, m2, cc, sc):
    return pl.pallas_call(
        _dft2_body,
        grid=(n_ka,),
        in_specs=[pl.BlockSpec((None, t2, FOURIER_W), lambda ka: (ka, 0, 0)),
                  pl.BlockSpec((None, t2, FOURIER_W), lambda ka: (im_off + ka, 0, 0)),
                  pl.BlockSpec((None, t2, 1), lambda ka: (ka, 0, 0)),
                  pl.BlockSpec((None, t2, 1), lambda ka: (ka, 0, 0)),
                  pl.BlockSpec((2 * t2, 2 * t2), lambda ka: (0, 0)),
                  pl.BlockSpec((FOURIER_CH, FOURIER_CH), lambda ka: (0, 0)),
                  pl.BlockSpec((FOURIER_CH, FOURIER_CH), lambda ka: (0, 0))],
        out_specs=pl.BlockSpec((t2, FOURIER_W), lambda ka: (0, ka)),
        out_shape=jax.ShapeDtypeStruct((t2, n_ka * FOURIER_W), BF16),
        compiler_params=_params(("arbitrary",), 32),
        name="dft_stage2",
    )(a3, a3, cw, sw, m2, cc, sc)


def _fourier(f, k):
    a = _dft_stage1(f, k["m1"]).reshape(2 * DFT_T1, DFT_T2, FOURIER_W)
    y_lat = _dft_stage2(a, DFT_T1, DFT_T2, DFT_T1, k["cw"], k["sw"], k["m2"], k["cc_l"], k["sc_l"])
    y_lat = y_lat.reshape(SEQ, FOURIER_W)
    fx = jnp.stack([f[SEQ:], jnp.zeros((CTX_LEN, FOURIER_W), f.dtype)])
    y_ctx = _dft_stage2(fx, 1, CTX_LEN, 1, k["cwx"], k["swx"], k["m2x"], k["cc_x"], k["sc_x"])
    return jnp.concatenate([y_lat, y_ctx], axis=0)


def _rope_tables():
    pos = np.arange(SEQ)
    ax = HEAD_DIM // 2
    inv_freq = 1.0 / (ROPE_THETA ** (np.arange(0, ax, 2, dtype=np.float32) / ax)).astype(np.float32)
    ang_r = (pos // GRID_W).astype(np.float32)[:, None] * inv_freq[None, :]
    ang_c = (pos % GRID_W).astype(np.float32)[:, None] * inv_freq[None, :]
    emb = np.concatenate([ang_r, ang_r, ang_c, ang_c], axis=-1).astype(np.float32)
    cos = np.cos(emb.astype(np.float64))
    sin = np.sin(emb.astype(np.float64))
    first = (np.arange(HEAD_DIM) % (HEAD_DIM // 2)) < HEAD_DIM // 4
    sin_lo = np.where(first[None, :], -sin, 0.0)
    sin_hi = np.where(first[None, :], 0.0, sin)
    pad = lambda a, v: np.concatenate([a, np.full((CTX_LEN, HEAD_DIM), v)], axis=0)
    return (jnp.asarray(pad(cos, 1.0), F32), jnp.asarray(pad(sin_lo, 0.0), F32),
            jnp.asarray(pad(sin_hi, 0.0), F32))


def kernel(x, c, ctx, c_ctx, ada_down, ada_up, ada_b, norm_g, ffn_wi, ffn_wo, w_in, q_norm, k_norm,
           sink, sgu_ln_g, sgu_ln_b, sgu_w, sgu_b, conv_w, w_branch, w_out):
    assert x.shape == (1, SEQ, D_MODEL) and ctx.shape == (1, CTX_LEN, D_MODEL)
    h = jnp.concatenate([x[0], ctx[0]], axis=0)
    cv = jnp.concatenate([c, c_ctx[None, :], jnp.zeros((6, D_MODEL), F32)], axis=0)
    m_all = _ada_mod(cv, ada_down, ada_up, ada_b)
    cos, sin_lo, sin_hi = _rope_tables()
    dft = _dft_consts()

    wb4 = w_branch.reshape(DEPTH, N_BRANCH * BRANCH_W, D_MODEL)

    def jobs_for(l):
        return dict(
            wi0=_CastJob(ffn_wi, (l, 0), D_MODEL, 0, 2 * D_FF, 512, 1024),
            wi1=_CastJob(ffn_wi, (l, 1), D_MODEL, 0, 2 * D_FF, 512, 1024),
            wo0=_CastJob(ffn_wo, (l, 0), D_FF, 0, D_MODEL, 512, 1024),
            wo1=_CastJob(ffn_wo, (l, 1), D_FF, 0, D_MODEL, 512, 1024),
            wic=_CastJob(w_in, (l,), D_MODEL, 0, COL_GATES, 1024, 512),
            wig=_CastJob(w_in, (l,), D_MODEL, COL_GATES, N_BRANCH * D_MODEL, 1024, 512),
            wb=_CastJob(wb4, (l,), N_BRANCH * BRANCH_W, 0, D_MODEL, 512, 1024),
            wout=_CastJob(w_out, (l,), D_MODEL, 0, D_MODEL, 512, 1024))

    wts = {k: _cast_now(j) for k, j in jobs_for(0).items()}
    for l in range(DEPTH):
        m = m_all[l]
        nxt = jobs_for(l + 1) if l + 1 < DEPTH else {}
        new = {}

        def run(call, *ks):
            ks = [k for k in ks if k in nxt]
            out, cast = call([nxt[k] for k in ks])
            new.update(zip(ks, cast))
            return out

        z = _norm_mod(h, norm_g[l, 0][None, :], m, 0)
        g = run(lambda jb: _mm_swiglu(z, wts["wi0"], jb), "wi0")
        h = run(lambda jb: _mm_resid(g, wts["wo0"], h, m, 0, 0.5, jb), "wo0")
        z = _norm_mod(h, norm_g[l, 1][None, :], m, 1)
        p = run(lambda jb: _mm_cols(z, wts["wic"], 0, COL_FOURIER, jb), "wic")
        f, _ = _mm_cols(z, wts["wic"], COL_FOURIER, FOURIER_W)
        k_rot = _kprep(p, k_norm[l][None, :], cos, sin_lo, sin_hi)
        ys = (_attention(p, k_rot, q_norm[l][None, :], sink[l], cos, sin_lo, sin_hi),
              _sgu(p, sgu_ln_g[l][None, :], sgu_ln_b[l][None, :], sgu_w[l].astype(BF16),
                   sgu_b[l].reshape(SGU_W // 512, 512 // SGU_CH, CHUNK).transpose(0, 2, 1)),
              _short_conv(p, conv_w[l]),
              _fourier(f, dft))
        merged = run(lambda jb: _merge(z, ys, wts["wig"], wts["wb"], jb), "wig")
        h = run(lambda jb: _mm_resid(merged, wts["wout"], h, m, 1, 1.0, jb), "wout")
        z = _norm_mod(h, norm_g[l, 2][None, :], m, 2)
        g = run(lambda jb: _mm_swiglu(z, wts["wi1"], jb), "wi1", "wb")
        h = run(lambda jb: _mm_resid(g, wts["wo1"], h, m, 2, 0.5, jb), "wo1")
        wts = new
    return h[:SEQ][None]
```

```python
import functools
from typing import NamedTuple

import numpy as np
import jax
import jax.numpy as jnp
from jax import lax
from jax.experimental import pallas as pl
from jax.experimental.pallas import tpu as pltpu

D_MODEL = 4096
SEQ = 8192
DEPTH = 4
GRID_W = 64
CTX_LEN = 256
S_ALL = SEQ + CTX_LEN
HEAD_DIM = 128
N_Q_HEADS = 8
N_KV_HEADS = 2
GQA_GROUP = N_Q_HEADS // N_KV_HEADS
Q_W = N_Q_HEADS * HEAD_DIM
KV_W = N_KV_HEADS * HEAD_DIM
WINDOW = 128
ATTN_BLOCK = 128
ROPE_THETA = 10000.0
SGU_GROUPS = 8
SGU_CH = 128
SGU_W = SGU_GROUPS * SGU_CH
CHUNK = 128
CONV_W = 1024
FOURIER_GROUPS = 4
FOURIER_CH = 256
FOURIER_W = FOURIER_GROUPS * FOURIER_CH
N_BRANCH = 4
BRANCH_W = 1024
D_FF = 3584
ADA_RANK = 256
N_MOD = 9
NORM_EPS = 1e-6
MASK_VALUE = -1e30

COL_Q = 0
COL_K = Q_W
COL_V = COL_K + KV_W
COL_SGU_U = COL_V + KV_W
COL_SGU_V = COL_SGU_U + SGU_W
COL_CONV_X = COL_SGU_V + SGU_W
COL_CONV_B = COL_CONV_X + CONV_W
COL_CONV_C = COL_CONV_B + CONV_W
COL_FOURIER = COL_CONV_C + CONV_W
COL_GATES = COL_FOURIER + FOURIER_W
IN_COLS = COL_GATES + N_BRANCH * D_MODEL

BM = 768
BM_WIDE = 1408
BR = 256
NORM_ROWS = 16
DFT_T1 = 64
DFT_T2 = 128
MXU_N = 256
MIB = 1024 * 1024

F32 = jnp.float32
BF16 = jnp.bfloat16


def _params(sem, vmem_mib):
    return pltpu.CompilerParams(dimension_semantics=sem, vmem_limit_bytes=vmem_mib * MIB)


def _dot(a, b):
    return jnp.dot(a, b, preferred_element_type=F32)


def _dot_nt(a, b):
    return lax.dot_general(a, b, (((1,), (1,)), ((), ())), preferred_element_type=F32)


def _sigmoid(x):
    return 0.5 * jnp.tanh(0.5 * x) + 0.5


class _CastJob(NamedTuple):
    src: jax.Array
    lead: tuple
    rows: int
    col0: int
    ncols: int
    rb: int
    cb: int

    @property
    def nblk(self):
        return (self.rows // self.rb) * (self.ncols // self.cb)


def _cast_specs(job, nj):
    nbc = job.ncols // job.cb
    c0 = job.col0 // job.cb
    last = job.nblk - 1

    def src_map(i, j):
        tb = jnp.minimum(i * nj + j, last)
        return job.lead + (tb // nbc, c0 + tb % nbc)

    def dst_map(i, j):
        tb = jnp.minimum(i * nj + j, last)
        return (tb // nbc, tb % nbc)

    return (pl.BlockSpec((None,) * len(job.lead) + (job.rb, job.cb), src_map),
            pl.BlockSpec((job.rb, job.cb), dst_map),
            jax.ShapeDtypeStruct((job.rows, job.ncols), BF16))


def _host_call(body, grid, in_specs, out_spec, out_shape, args, jobs, vmem_mib, name):
    n_in, n_jobs = len(in_specs), len(jobs)
    assert all(j.nblk <= grid[0] * grid[1] for j in jobs)
    specs = [_cast_specs(j, grid[1]) for j in jobs]

    def wrapped(*refs):
        body(*refs[:n_in], refs[n_in + n_jobs])
        for s_ref, d_ref in zip(refs[n_in:n_in + n_jobs], refs[n_in + n_jobs + 1:]):
            d_ref[...] = s_ref[...].astype(BF16)

    outs = pl.pallas_call(
        wrapped,
        grid=grid,
        in_specs=list(in_specs) + [s[0] for s in specs],
        out_specs=[out_spec] + [s[1] for s in specs],
        out_shape=[out_shape] + [s[2] for s in specs],
        compiler_params=_params(("arbitrary", "arbitrary"), vmem_mib),
        name=name,
    )(*args, *[j.src for j in jobs])
    return outs[0], list(outs[1:])


def _cast_body(s_ref, d_ref):
    d_ref[...] = s_ref[...].astype(BF16)


def _cast_now(job):
    src_spec, dst_spec, shape = _cast_specs(job, 1)
    return pl.pallas_call(
        _cast_body,
        grid=(job.nblk, 1),
        in_specs=[src_spec],
        out_specs=dst_spec,
        out_shape=shape,
        compiler_params=_params(("arbitrary", "arbitrary"), 32),
        name="cast_bf16",
    )(job.src)


def _ada_down_body(cv_ref, down_ref, o_ref):
    cv = cv_ref[...]
    s = (cv * jax.nn.sigmoid(cv)).astype(BF16)
    o_ref[...] = _dot(s, down_ref[...].astype(BF16))


def _ada_up_body(t_ref, up_ref, b_ref, o_ref):
    o_ref[...] = _dot(t_ref[...].astype(BF16), up_ref[...].astype(BF16)) + b_ref[...]


def _ada_mod(cv, ada_down, ada_up, ada_b):
    t = pl.pallas_call(
        _ada_down_body,
        grid=(DEPTH,),
        in_specs=[pl.BlockSpec((8, D_MODEL), lambda l: (0, 0)),
                  pl.BlockSpec((None, D_MODEL, ADA_RANK), lambda l: (l, 0, 0))],
        out_specs=pl.BlockSpec((None, 8, ADA_RANK), lambda l: (l, 0, 0)),
        out_shape=jax.ShapeDtypeStruct((DEPTH, 8, ADA_RANK), F32),
        compiler_params=_params(("arbitrary",), 32),
        name="ada_down",
    )(cv, ada_down)
    bn = 4608
    nj = N_MOD * D_MODEL // bn
    return pl.pallas_call(
        _ada_up_body,
        grid=(DEPTH, nj),
        in_specs=[pl.BlockSpec((None, 8, ADA_RANK), lambda l, j: (l, 0, 0)),
                  pl.BlockSpec((None, ADA_RANK, bn), lambda l, j: (l, 0, j)),
                  pl.BlockSpec((None, 1, bn), lambda l, j: (l, 0, j))],
        out_specs=pl.BlockSpec((None, 8, bn), lambda l, j: (l, 0, j)),
        out_shape=jax.ShapeDtypeStruct((DEPTH, 8, N_MOD * D_MODEL), F32),
        compiler_params=_params(("arbitrary", "arbitrary"), 32),
        name="ada_up",
    )(t, ada_up, ada_b.reshape(DEPTH, 1, N_MOD * D_MODEL))


def _norm_mod_body(h_ref, g_ref, shift_ref, scale_ref, o_ref):
    is_ctx = pl.program_id(0) >= SEQ // BR
    scale = jnp.where(is_ctx, scale_ref[1:2, :], scale_ref[0:1, :])
    shift = jnp.where(is_ctx, shift_ref[1:2, :], shift_ref[0:1, :])
    gain = g_ref[...] * (1.0 + scale)
    for r0 in range(0, BR, NORM_ROWS):
        x = h_ref[r0:r0 + NORM_ROWS, :]
        r = lax.rsqrt(jnp.mean(x * x, axis=-1, keepdims=True) + NORM_EPS)
        o_ref[r0:r0 + NORM_ROWS, :] = ((x * r) * gain + shift).astype(o_ref.dtype)


def _norm_mod(h, g_row, m, i_mod):
    return pl.pallas_call(
        _norm_mod_body,
        grid=(S_ALL // BR,),
        in_specs=[pl.BlockSpec((BR, D_MODEL), lambda i: (i, 0)),
                  pl.BlockSpec((1, D_MODEL), lambda i: (0, 0)),
                  pl.BlockSpec((8, D_MODEL), lambda i: (0, 3 * i_mod)),
                  pl.BlockSpec((8, D_MODEL), lambda i: (0, 3 * i_mod + 1))],
        out_specs=pl.BlockSpec((BR, D_MODEL), lambda i: (i, 0)),
        out_shape=jax.ShapeDtypeStruct((S_ALL, D_MODEL), BF16),
        compiler_params=_params(("arbitrary",), 32),
        name="norm_mod",
    )(h, g_row, m, m)


def _col_chunks(ref):
    return [slice(c, c + MXU_N) for c in range(0, ref.shape[1], MXU_N)]


def _swiglu_body(z_ref, wa_ref, wb_ref, o_ref):
    z = z_ref[...]
    for sl in _col_chunks(o_ref):
        a = _dot(z, wa_ref[:, sl])
        b = _dot(z, wb_ref[:, sl])
        o_ref[:, sl] = (a * _sigmoid(a) * b).astype(o_ref.dtype)


def _mm_swiglu(z, wi, jobs=()):
    bn = 512
    nj = D_FF // bn
    return _host_call(
        _swiglu_body, (S_ALL // BM, nj),
        [pl.BlockSpec((BM, D_MODEL), lambda i, j: (i, 0)),
         pl.BlockSpec((D_MODEL, bn), lambda i, j: (0, j)),
         pl.BlockSpec((D_MODEL, bn), lambda i, j: (0, j + nj))],
        pl.BlockSpec((BM, bn), lambda i, j: (i, j)),
        jax.ShapeDtypeStruct((S_ALL, D_FF), BF16),
        (z, wi, wi), jobs, 56, "mm_swiglu")


def _resid_body(x_ref, w_ref, h_ref, gate_ref, o_ref, *, coef):
    x = x_ref[...]
    bm = x_ref.shape[0]
    is_ctx = pl.program_id(0) * bm + lax.broadcasted_iota(jnp.int32, (bm, 1), 0) >= SEQ
    for sl in _col_chunks(o_ref):
        gate = jnp.where(is_ctx, gate_ref[1:2, sl], gate_ref[0:1, sl])
        o_ref[:, sl] = h_ref[:, sl] + (coef * gate) * _dot(x, w_ref[:, sl])


def _mm_resid(xin, w, h, m, i_mod, coef, jobs=()):
    k = xin.shape[1]
    bm, bn = BM_WIDE, 512
    nj = D_MODEL // bn
    return _host_call(
        functools.partial(_resid_body, coef=coef), (S_ALL // bm, nj),
        [pl.BlockSpec((bm, k), lambda i, j: (i, 0)),
         pl.BlockSpec((k, bn), lambda i, j: (0, j)),
         pl.BlockSpec((bm, bn), lambda i, j: (i, j)),
         pl.BlockSpec((8, bn), lambda i, j: (0, (3 * i_mod + 2) * nj + j))],
        pl.BlockSpec((bm, bn), lambda i, j: (i, j)),
        jax.ShapeDtypeStruct((S_ALL, D_MODEL), F32),
        (xin, w, h, m), jobs, 56, "mm_resid")


def _plain_body(z_ref, w_ref, o_ref):
    z = z_ref[...]
    for sl in _col_chunks(o_ref):
        o_ref[:, sl] = _dot(z, w_ref[:, sl]).astype(o_ref.dtype)


def _mm_cols(z, w, col0, ncols, jobs=()):
    bm, bn = BM_WIDE, 512
    j0 = col0 // bn
    return _host_call(
        _plain_body, (S_ALL // bm, ncols // bn),
        [pl.BlockSpec((bm, D_MODEL), lambda i, j: (i, 0)),
         pl.BlockSpec((D_MODEL, bn), lambda i, j: (0, j0 + j))],
        pl.BlockSpec((bm, bn), lambda i, j: (i, j)),
        jax.ShapeDtypeStruct((S_ALL, ncols), BF16),
        (z, w), jobs, 56, "mm_cols")


def _merge_body(z_ref, y0, y1, y2, y3, g0, g1, g2, g3, b0, b1, b2, b3, o_ref):
    z = z_ref[...]
    acc = None
    for y_ref, g_ref, b_ref in ((y0, g0, b0), (y1, g1, b1), (y2, g2, b2), (y3, g3, b3)):
        t = _sigmoid(_dot(z, g_ref[...])) * _dot(y_ref[...], b_ref[...])
        acc = t if acc is None else acc + t
    o_ref[...] = acc.astype(o_ref.dtype)


def _merge(z, ys, w_gates, w_branch, jobs=()):
    bn = 256
    nj = D_MODEL // bn
    z_spec = pl.BlockSpec((BM, D_MODEL), lambda i, j: (i, 0))
    y_spec = pl.BlockSpec((BM, BRANCH_W), lambda i, j: (i, 0))
    g_specs = [pl.BlockSpec((D_MODEL, bn), functools.partial(lambda i, j, r: (0, r * nj + j), r=r))
               for r in range(N_BRANCH)]
    b_specs = [pl.BlockSpec((BRANCH_W, bn), functools.partial(lambda i, j, r: (r, j), r=r))
               for r in range(N_BRANCH)]
    return _host_call(
        _merge_body, (S_ALL // BM, nj),
        [z_spec] + [y_spec] * N_BRANCH + g_specs + b_specs,
        pl.BlockSpec((BM, bn), lambda i, j: (i, j)),
        jax.ShapeDtypeStruct((S_ALL, D_MODEL), BF16),
        (z, *ys, *([w_gates] * N_BRANCH), *([w_branch] * N_BRANCH)), jobs, 60, "merge")


def _head_norm_rope(t, g, cos, sin_lo, sin_hi):
    t = t * lax.rsqrt(jnp.mean(t * t, axis=-1, keepdims=True) + NORM_EPS) * g
    return t * cos + pltpu.roll(t, 96, 1) * sin_lo + pltpu.roll(t, 32, 1) * sin_hi


def _kprep_body(k_ref, g_ref, cos_ref, lo_ref, hi_ref, o_ref):
    for hk in range(N_KV_HEADS):
        sl = slice(hk * HEAD_DIM, (hk + 1) * HEAD_DIM)
        t = _head_norm_rope(k_ref[:, sl].astype(F32), g_ref[...], cos_ref[...], lo_ref[...], hi_ref[...])
        o_ref[:, sl] = t.astype(o_ref.dtype)


def _kprep(p, k_g, cos, sin_lo, sin_hi):
    tab = pl.BlockSpec((BR, HEAD_DIM), lambda i: (i, 0))
    return pl.pallas_call(
        _kprep_body,
        grid=(S_ALL // BR,),
        in_specs=[pl.BlockSpec((BR, KV_W), lambda i: (i, COL_K // KV_W)),
                  pl.BlockSpec((1, HEAD_DIM), lambda i: (0, 0)), tab, tab, tab],
        out_specs=pl.BlockSpec((BR, KV_W), lambda i: (i, 0)),
        out_shape=jax.ShapeDtypeStruct((S_ALL, KV_W), BF16),
        compiler_params=_params(("arbitrary",), 32),
        name="kprep",
    )(p, k_g, cos, sin_lo, sin_hi)


def _attn_body(sink_ref, q_ref, g_ref, cos_ref, lo_ref, hi_ref,
               kp_ref, kc_ref, kn_ref, kx_ref, vp_ref, vc_ref, vn_ref, vx_ref, o_ref):
    n = pl.program_id(0)
    nb_lat = SEQ // ATTN_BLOCK
    rows = GQA_GROUP * ATTN_BLOCK
    n_loc = 3 * ATTN_BLOCK
    is_lat = n < nb_lat
    lo = jnp.where(is_lat, jnp.where(n >= 1, 0, ATTN_BLOCK), n_loc)
    hi = jnp.where(n < nb_lat - 1, n_loc, 2 * ATTN_BLOCK)
    r = lax.broadcasted_iota(jnp.int32, (rows, n_loc), 0) & (ATTN_BLOCK - 1)
    a = lax.broadcasted_iota(jnp.int32, (rows, n_loc), 1)
    valid = (a >= r) & (a <= r + 2 * WINDOW) & (a >= lo) & (a < hi)
    row_head = lax.broadcasted_iota(jnp.int32, (rows, 1), 0) // ATTN_BLOCK
    scale = HEAD_DIM ** -0.5
    for hk in range(N_KV_HEADS):
        sl = slice(hk * HEAD_DIM, (hk + 1) * HEAD_DIM)
        qs = []
        for g in range(GQA_GROUP):
            h = hk * GQA_GROUP + g
            t = q_ref[:, h * HEAD_DIM:(h + 1) * HEAD_DIM].astype(F32)
            qs.append(_head_norm_rope(t, g_ref[...], cos_ref[...], lo_ref[...], hi_ref[...]).astype(BF16))
        q = jnp.concatenate(qs, axis=0)
        k_loc = jnp.concatenate([kp_ref[:, sl], kc_ref[:, sl], kn_ref[:, sl]], axis=0)
        v_loc = jnp.concatenate([vp_ref[:, sl], vc_ref[:, sl], vn_ref[:, sl]], axis=0)
        s_loc = jnp.where(valid, _dot_nt(q, k_loc) * scale, MASK_VALUE)
        s_ctx = _dot_nt(q, kx_ref[:, sl]) * scale
        snk = jnp.zeros((rows, 1), F32)
        for g in range(GQA_GROUP):
            snk = jnp.where(row_head == g, sink_ref[hk * GQA_GROUP + g], snk)
        m = jnp.maximum(jnp.maximum(jnp.max(s_loc, axis=-1, keepdims=True),
                                    jnp.max(s_ctx, axis=-1, keepdims=True)), snk)
        p_loc = jnp.exp(s_loc - m)
        p_ctx = jnp.exp(s_ctx - m)
        den = (jnp.sum(p_loc, axis=-1, keepdims=True) + jnp.sum(p_ctx, axis=-1, keepdims=True)
               + jnp.exp(snk - m))
        o = (_dot(p_loc.astype(BF16), v_loc) + _dot(p_ctx.astype(BF16), vx_ref[:, sl])) / den
        for g in range(GQA_GROUP):
            h = hk * GQA_GROUP + g
            o_ref[:, h * HEAD_DIM:(h + 1) * HEAD_DIM] = o[g * ATTN_BLOCK:(g + 1) * ATTN_BLOCK].astype(o_ref.dtype)


def _attention(p, k_rot, q_g, sink, cos, sin_lo, sin_hi):
    nb = S_ALL // ATTN_BLOCK
    blk_ctx = SEQ // CTX_LEN
    tab = pl.BlockSpec((ATTN_BLOCK, HEAD_DIM), lambda n: (n, 0))
    prev = lambda n: jnp.maximum(n - 1, 0)
    nxt = lambda n: jnp.minimum(n + 1, nb - 1)
    jv = COL_V // KV_W
    return pl.pallas_call(
        _attn_body,
        grid=(nb,),
        in_specs=[pl.BlockSpec(memory_space=pltpu.SMEM),
                  pl.BlockSpec((ATTN_BLOCK, Q_W), lambda n: (n, 0)),
                  pl.BlockSpec((1, HEAD_DIM), lambda n: (0, 0)), tab, tab, tab,
                  pl.BlockSpec((ATTN_BLOCK, KV_W), lambda n: (prev(n), 0)),
                  pl.BlockSpec((ATTN_BLOCK, KV_W), lambda n: (n, 0)),
                  pl.BlockSpec((ATTN_BLOCK, KV_W), lambda n: (nxt(n), 0)),
                  pl.BlockSpec((CTX_LEN, KV_W), lambda n: (blk_ctx, 0)),
                  pl.BlockSpec((ATTN_BLOCK, KV_W), lambda n: (prev(n), jv)),
                  pl.BlockSpec((ATTN_BLOCK, KV_W), lambda n: (n, jv)),
                  pl.BlockSpec((ATTN_BLOCK, KV_W), lambda n: (nxt(n), jv)),
                  pl.BlockSpec((CTX_LEN, KV_W), lambda n: (blk_ctx, jv))],
        out_specs=pl.BlockSpec((ATTN_BLOCK, Q_W), lambda n: (n, 0)),
        out_shape=jax.ShapeDtypeStruct((S_ALL, Q_W), BF16),
        compiler_params=_params(("arbitrary",), 32),
        name="attention",
    )(sink, p, q_g, cos, sin_lo, sin_hi, k_rot, k_rot, k_rot, k_rot, p, p, p, p)


def _sgu_body(u_ref, v_ref, lg_ref, lb_ref, ws_ref, bs_ref, o_ref):
    half = u_ref.shape[1] // SGU_CH
    for g in range(half):
        sl = slice(g * SGU_CH, (g + 1) * SGU_CH)
        vg = jax.nn.gelu(v_ref[:, sl].astype(F32))
        mu = jnp.mean(vg, axis=-1, keepdims=True)
        d = vg - mu
        var = jnp.mean(d * d, axis=-1, keepdims=True)
        vn = (d * lax.rsqrt(var + NORM_EPS) * lg_ref[:, sl] + lb_ref[:, sl]).astype(BF16)
        for c in range(BR // CHUNK):
            rs = slice(c * CHUNK, (c + 1) * CHUNK)
            mixed = _dot(ws_ref[g], vn[rs]) + bs_ref[:, g:g + 1]
            o_ref[rs, sl] = (jax.nn.gelu(u_ref[rs, sl].astype(F32)) * mixed).astype(o_ref.dtype)


def _sgu(p, ln_g, ln_b, w_s, b_s_t):
    bw = 512
    gh = bw // SGU_CH
    ju, jv = COL_SGU_U // bw, COL_SGU_V // bw
    return pl.pallas_call(
        _sgu_body,
        grid=(S_ALL // BR, SGU_W // bw),
        in_specs=[pl.BlockSpec((BR, bw), lambda i, c: (i, ju + c)),
                  pl.BlockSpec((BR, bw), lambda i, c: (i, jv + c)),
                  pl.BlockSpec((1, bw), lambda i, c: (0, c)),
                  pl.BlockSpec((1, bw), lambda i, c: (0, c)),
                  pl.BlockSpec((gh, CHUNK, CHUNK), lambda i, c: (c, 0, 0)),
                  pl.BlockSpec((None, CHUNK, gh), lambda i, c: (c, 0, 0))],
        out_specs=pl.BlockSpec((BR, bw), lambda i, c: (i, c)),
        out_shape=jax.ShapeDtypeStruct((S_ALL, SGU_W), BF16),
        compiler_params=_params(("arbitrary", "arbitrary"), 32),
        name="sgu",
    )(p, p, ln_g, ln_b, w_s, b_s_t)


HALO = 16


def _conv_body(x_ref, b_ref, c_ref, xp_ref, cp_ref, xn_ref, cn_ref, w_ref, o_ref):
    i = pl.program_id(0)
    first_ctx = SEQ // BR
    has_prev = (i != 0) & (i != first_ctx)
    has_next = (i != first_ctx - 1) & (i != S_ALL // BR - 1)
    z = c_ref[...].astype(F32) * x_ref[...].astype(F32)
    zp = cp_ref[HALO - 1:HALO, :].astype(F32) * xp_ref[HALO - 1:HALO, :].astype(F32)
    zn = cn_ref[0:1, :].astype(F32) * xn_ref[0:1, :].astype(F32)
    zp = jnp.where(has_prev, zp, 0.0)
    zn = jnp.where(has_next, zn, 0.0)
    row = lax.broadcasted_iota(jnp.int32, (BR, 1), 0)
    z_prev = jnp.where(row == 0, zp, pltpu.roll(z, 1, 0))
    z_next = jnp.where(row == BR - 1, zn, pltpu.roll(z, BR - 1, 0))
    y = w_ref[0:1, :] * z_prev + w_ref[1:2, :] * z + w_ref[2:3, :] * z_next
    o_ref[...] = (b_ref[...].astype(F32) * y).astype(o_ref.dtype)


def _short_conv(p, conv_w):
    bw = 512
    jx, jb, jc = COL_CONV_X // bw, COL_CONV_B // bw, COL_CONV_C // bw
    rb = BR // HALO
    nh = S_ALL // HALO
    prev = lambda i: jnp.maximum(i * rb - 1, 0)
    nxt = lambda i: jnp.minimum((i + 1) * rb, nh - 1)
    return pl.pallas_call(
        _conv_body,
        grid=(S_ALL // BR, CONV_W // bw),
        in_specs=[pl.BlockSpec((BR, bw), lambda i, c: (i, jx + c)),
                  pl.BlockSpec((BR, bw), lambda i, c: (i, jb + c)),
                  pl.BlockSpec((BR, bw), lambda i, c: (i, jc + c)),
                  pl.BlockSpec((HALO, bw), lambda i, c: (prev(i), jx + c)),
                  pl.BlockSpec((HALO, bw), lambda i, c: (prev(i), jc + c)),
                  pl.BlockSpec((HALO, bw), lambda i, c: (nxt(i), jx + c)),
                  pl.BlockSpec((HALO, bw), lambda i, c: (nxt(i), jc + c)),
                  pl.BlockSpec((3, bw), lambda i, c: (0, c))],
        out_specs=pl.BlockSpec((BR, bw), lambda i, c: (i, c)),
        out_shape=jax.ShapeDtypeStruct((S_ALL, CONV_W), BF16),
        compiler_params=_params(("arbitrary", "arbitrary"), 32),
        name="short_conv",
    )(p, p, p, p, p, p, p, conv_w)


def _dft_consts():
    def cs(n, rows, cols):
        ang = 2.0 * np.pi * ((np.outer(rows, cols)) % n) / n
        return np.cos(ang), np.sin(ang)

    t1, t2 = DFT_T1, DFT_T2
    c1, s1 = cs(t1, np.arange(t1), np.arange(t1))
    m1 = np.concatenate([c1, -s1], axis=0)
    cw, sw = cs(SEQ, np.arange(t1), np.arange(t2))
    c2, s2 = cs(t2, np.arange(t2), np.arange(t2))
    m2 = np.block([[c2, s2], [-s2, c2]])
    cx, sx = cs(CTX_LEN, np.arange(CTX_LEN), np.arange(CTX_LEN))
    m2x = np.block([[cx, sx], [-sx, cx]])
    cc, sc = cs(FOURIER_CH, np.arange(FOURIER_CH), np.arange(FOURIER_CH))
    nl = 1.0 / np.sqrt(SEQ * FOURIER_CH)
    nx = 1.0 / np.sqrt(CTX_LEN * FOURIER_CH)
    as_bf = lambda a: jnp.asarray(a, dtype=F32).astype(BF16)
    return dict(
        m1=as_bf(m1), m2=as_bf(m2), m2x=as_bf(m2x),
        cw=jnp.asarray(cw[:, :, None], F32), sw=jnp.asarray(sw[:, :, None], F32),
        cwx=jnp.ones((1, CTX_LEN, 1), F32), swx=jnp.zeros((1, CTX_LEN, 1), F32),
        cc_l=as_bf(cc * nl), sc_l=as_bf(sc * nl), cc_x=as_bf(cc * nx), sc_x=as_bf(sc * nx))


def _dft1_body(m_ref, x_ref, o_ref):
    o_ref[...] = _dot(m_ref[...], x_ref[...]).astype(o_ref.dtype)


def _dft_stage1(f, m1):
    ncol = DFT_T2 * FOURIER_W
    bn = 8192
    fv = f.reshape(S_ALL // DFT_T2, ncol)
    return pl.pallas_call(
        _dft1_body,
        grid=(ncol // bn,),
        in_specs=[pl.BlockSpec((2 * DFT_T1, DFT_T1), lambda j: (0, 0)),
                  pl.BlockSpec((DFT_T1, bn), lambda j: (0, j))],
        out_specs=pl.BlockSpec((2 * DFT_T1, bn), lambda j: (0, j)),
        out_shape=jax.ShapeDtypeStruct((2 * DFT_T1, ncol), BF16),
        compiler_params=_params(("arbitrary",), 32),
        name="dft_stage1",
    )(m1, fv)


def _dft2_body(ar_ref, ai_ref, cw_ref, sw_ref, m2_ref, cc_ref, sc_ref, o_ref):
    n_grp, t2 = ar_ref.shape[0], ar_ref.shape[1]
    for k in range(n_grp):
        ar = ar_ref[k].astype(F32)
        ai = ai_ref[k].astype(F32)
        cw = cw_ref[k]
        sw = sw_ref[k]
        b = jnp.concatenate([ar * cw + ai * sw, ai * cw - ar * sw], axis=0).astype(BF16)
        p = _dot(m2_ref[...], b)
        pr = p[:t2].astype(BF16)
        pi = p[t2:].astype(BF16)
        for g in range(FOURIER_GROUPS):
            sl = slice(g * FOURIER_CH, (g + 1) * FOURIER_CH)
            y = _dot(pr[:, sl], cc_ref[...]) + _dot(pi[:, sl], sc_ref[...])
            o_ref[:, k * FOURIER_W + g * FOURIER_CH:k * FOURIER_W + (g + 1) * FOURIER_CH] = y.astype(o_ref.dtype)


def _dft_stage2(a3, n_ka, t2, im_off, cw, sw, m2, cc, sc):
    grp = min(n_ka, 4)
    io = im_off // grp
    return pl.pallas_call(
        _dft2_body,
        grid=(n_ka // grp,),
        in_specs=[pl.BlockSpec((grp, t2, FOURIER_W), lambda ka: (ka, 0, 0)),
                  pl.BlockSpec((grp, t2, FOURIER_W), lambda ka: (io + ka, 0, 0)),
                  pl.BlockSpec((grp, t2, 1), lambda ka: (ka, 0, 0)),
                  pl.BlockSpec((grp, t2, 1), lambda ka: (ka, 0, 0)),
                  pl.BlockSpec((2 * t2, 2 * t2), lambda ka: (0, 0)),
                  pl.BlockSpec((FOURIER_CH, FOURIER_CH), lambda ka: (0, 0)),
                  pl.BlockSpec((FOURIER_CH, FOURIER_CH), lambda ka: (0, 0))],
        out_specs=pl.BlockSpec((t2, grp * FOURIER_W), lambda ka: (0, ka)),
        out_shape=jax.ShapeDtypeStruct((t2, n_ka * FOURIER_W), BF16),
        compiler_params=_params(("arbitrary",), 32),
        name="dft_stage2",
    )(a3, a3, cw, sw, m2, cc, sc)


def _fourier(f, k):
    a = _dft_stage1(f, k["m1"]).reshape(2 * DFT_T1, DFT_T2, FOURIER_W)
    y_lat = _dft_stage2(a, DFT_T1, DFT_T2, DFT_T1, k["cw"], k["sw"], k["m2"], k["cc_l"], k["sc_l"])
    y_lat = y_lat.reshape(SEQ, FOURIER_W)
    fx = jnp.stack([f[SEQ:], jnp.zeros((CTX_LEN, FOURIER_W), f.dtype)])
    y_ctx = _dft_stage2(fx, 1, CTX_LEN, 1, k["cwx"], k["swx"], k["m2x"], k["cc_x"], k["sc_x"])
    return jnp.concatenate([y_lat, y_ctx], axis=0)


def _rope_tables():
    pos = np.arange(SEQ)
    ax = HEAD_DIM // 2
    inv_freq = 1.0 / (ROPE_THETA ** (np.arange(0, ax, 2, dtype=np.float32) / ax)).astype(np.float32)
    ang_r = (pos // GRID_W).astype(np.float32)[:, None] * inv_freq[None, :]
    ang_c = (pos % GRID_W).astype(np.float32)[:, None] * inv_freq[None, :]
    emb = np.concatenate([ang_r, ang_r, ang_c, ang_c], axis=-1).astype(np.float32)
    cos = np.cos(emb.astype(np.float64))
    sin = np.sin(emb.astype(np.float64))
    first = (np.arange(HEAD_DIM) % (HEAD_DIM // 2)) < HEAD_DIM // 4
    sin_lo = np.where(first[None, :], -sin, 0.0)
    sin_hi = np.where(first[None, :], 0.0, sin)
    pad = lambda a, v: np.concatenate([a, np.full((CTX_LEN, HEAD_DIM), v)], axis=0)
    return (jnp.asarray(pad(cos, 1.0), F32), jnp.asarray(pad(sin_lo, 0.0), F32),
            jnp.asarray(pad(sin_hi, 0.0), F32))


def kernel(x, c, ctx, c_ctx, ada_down, ada_up, ada_b, norm_g, ffn_wi, ffn_wo, w_in, q_norm, k_norm,
           sink, sgu_ln_g, sgu_ln_b, sgu_w, sgu_b, conv_w, w_branch, w_out):
    assert x.shape == (1, SEQ, D_MODEL) and ctx.shape == (1, CTX_LEN, D_MODEL)
    h = jnp.concatenate([x[0], ctx[0]], axis=0)
    cv = jnp.concatenate([c, c_ctx[None, :], jnp.zeros((6, D_MODEL), F32)], axis=0)
    m_all = _ada_mod(cv, ada_down, ada_up, ada_b)
    cos, sin_lo, sin_hi = _rope_tables()
    dft = _dft_consts()

    wb4 = w_branch.reshape(DEPTH, N_BRANCH * BRANCH_W, D_MODEL)

    def job(l, name):
        big = (512, 1024)
        gate_rb = 2048 if l == 0 else 1024
        return dict(
            wi0=lambda: _CastJob(ffn_wi, (l, 0), D_MODEL, 0, 2 * D_FF, *big),
            wi1=lambda: _CastJob(ffn_wi, (l, 1), D_MODEL, 0, 2 * D_FF, *big),
            wo0=lambda: _CastJob(ffn_wo, (l, 0), D_FF, 0, D_MODEL, *big),
            wo1=lambda: _CastJob(ffn_wo, (l, 1), D_FF, 0, D_MODEL, *big),
            wic=lambda: _CastJob(w_in, (l,), D_MODEL, 0, COL_GATES, 1024, 512),
            wig=lambda: _CastJob(w_in, (l,), D_MODEL, COL_GATES, N_BRANCH * D_MODEL, gate_rb, 512),
            wb=lambda: _CastJob(wb4, (l,), N_BRANCH * BRANCH_W, 0, D_MODEL, *big),
            wout=lambda: _CastJob(w_out, (l,), D_MODEL, 0, D_MODEL, *big))[name]()

    def rides(call, l):
        nxt = {"swiglu1": ["wi0"], "resid1": ["wo0"], "cols": ["wic"], "merge": ["wig"],
               "resid_out": ["wout"], "swiglu2": ["wi1", "wb"], "resid2": ["wo1"]}[call]
        keys = [(l + 1, n) for n in nxt] if l + 1 < DEPTH else []
        if l == 0:
            keys += [(0, n) for n in {"swiglu1": ["wig"], "resid1": ["wout"], "cols": ["wb", "wi1"],
                                      "swiglu2": ["wo1"]}.get(call, [])]
        return keys

    wts = {(0, n): _cast_now(job(0, n)) for n in ("wi0", "wo0", "wic")}
    for l in range(DEPTH):
        m = m_all[l]

        def run(call, fn):
            keys = rides(call, l)
            out, cast = fn([job(*k) for k in keys])
            wts.update(zip(keys, cast))
            return out

        z = _norm_mod(h, norm_g[l, 0][None, :], m, 0)
        g = run("swiglu1", lambda jb: _mm_swiglu(z, wts[l, "wi0"], jb))
        h = run("resid1", lambda jb: _mm_resid(g, wts[l, "wo0"], h, m, 0, 0.5, jb))
        z = _norm_mod(h, norm_g[l, 1][None, :], m, 1)
        p = run("cols", lambda jb: _mm_cols(z, wts[l, "wic"], 0, COL_FOURIER, jb))
        f, _ = _mm_cols(z, wts[l, "wic"], COL_FOURIER, FOURIER_W)
        k_rot = _kprep(p, k_norm[l][None, :], cos, sin_lo, sin_hi)
        ys = (_attention(p, k_rot, q_norm[l][None, :], sink[l], cos, sin_lo, sin_hi),
              _sgu(p, sgu_ln_g[l][None, :], sgu_ln_b[l][None, :], sgu_w[l].astype(BF16),
                   sgu_b[l].reshape(SGU_W // 512, 512 // SGU_CH, CHUNK).transpose(0, 2, 1)),
              _short_conv(p, conv_w[l]),
              _fourier(f, dft))
        merged = run("merge", lambda jb: _merge(z, ys, wts[l, "wig"], wts[l, "wb"], jb))
        h = run("resid_out", lambda jb: _mm_resid(merged, wts[l, "wout"], h, m, 1, 1.0, jb))
        z = _norm_mod(h, norm_g[l, 2][None, :], m, 2)
        g = run("swiglu2", lambda jb: _mm_swiglu(z, wts[l, "wi1"], jb))
        h = run("resid2", lambda jb: _mm_resid(g, wts[l, "wo1"], h, m, 2, 0.5, jb))
    return h[:SEQ][None]
```

```python
import functools
from typing import NamedTuple

import numpy as np
import jax
import jax.numpy as jnp
from jax import lax
from jax.experimental import pallas as pl
from jax.experimental.pallas import tpu as pltpu

D_MODEL = 4096
SEQ = 8192
DEPTH = 4
GRID_W = 64
CTX_LEN = 256
S_ALL = SEQ + CTX_LEN
HEAD_DIM = 128
N_Q_HEADS = 8
N_KV_HEADS = 2
GQA_GROUP = N_Q_HEADS // N_KV_HEADS
Q_W = N_Q_HEADS * HEAD_DIM
KV_W = N_KV_HEADS * HEAD_DIM
WINDOW = 128
ATTN_BLOCK = 128
ROPE_THETA = 10000.0
SGU_GROUPS = 8
SGU_CH = 128
SGU_W = SGU_GROUPS * SGU_CH
CHUNK = 128
CONV_W = 1024
FOURIER_GROUPS = 4
FOURIER_CH = 256
FOURIER_W = FOURIER_GROUPS * FOURIER_CH
N_BRANCH = 4
BRANCH_W = 1024
D_FF = 3584
ADA_RANK = 256
N_MOD = 9
NORM_EPS = 1e-6
MASK_VALUE = -1e30

COL_Q = 0
COL_K = Q_W
COL_V = COL_K + KV_W
COL_SGU_U = COL_V + KV_W
COL_SGU_V = COL_SGU_U + SGU_W
COL_CONV_X = COL_SGU_V + SGU_W
COL_CONV_B = COL_CONV_X + CONV_W
COL_CONV_C = COL_CONV_B + CONV_W
COL_FOURIER = COL_CONV_C + CONV_W
COL_GATES = COL_FOURIER + FOURIER_W
IN_COLS = COL_GATES + N_BRANCH * D_MODEL

BM = 768
BM_WIDE = 1408
BM_LATENT = 2048
BS = 768
NORM_ROWS = 16
DFT_T1 = 64
DFT_T2 = 128
MXU_N = 256
MIB = 1024 * 1024

F32 = jnp.float32
BF16 = jnp.bfloat16


def _params(sem, vmem_mib):
    return pltpu.CompilerParams(dimension_semantics=sem, vmem_limit_bytes=vmem_mib * MIB)


def _dot(a, b):
    return jnp.dot(a, b, preferred_element_type=F32)


def _dot_nt(a, b):
    return lax.dot_general(a, b, (((1,), (1,)), ((), ())), preferred_element_type=F32)


def _sigmoid(x):
    return 0.5 * jnp.tanh(0.5 * x) + 0.5


class _CastJob(NamedTuple):
    src: jax.Array
    lead: tuple
    rows: int
    col0: int
    ncols: int
    rb: int
    cb: int

    @property
    def nblk(self):
        return (self.rows // self.rb) * (self.ncols // self.cb)


def _cast_specs(job, nj):
    nbc = job.ncols // job.cb
    c0 = job.col0 // job.cb
    last = job.nblk - 1

    def src_map(i, j):
        tb = jnp.minimum(i * nj + j, last)
        return job.lead + (tb // nbc, c0 + tb % nbc)

    def dst_map(i, j):
        tb = jnp.minimum(i * nj + j, last)
        return (tb // nbc, tb % nbc)

    return (pl.BlockSpec((None,) * len(job.lead) + (job.rb, job.cb), src_map),
            pl.BlockSpec((job.rb, job.cb), dst_map),
            jax.ShapeDtypeStruct((job.rows, job.ncols), BF16))


def _host_call(body, grid, in_specs, out_spec, out_shape, args, jobs, vmem_mib, name):
    n_in, n_jobs = len(in_specs), len(jobs)
    assert all(j.nblk <= grid[0] * grid[1] for j in jobs)
    specs = [_cast_specs(j, grid[1]) for j in jobs]

    def wrapped(*refs):
        body(*refs[:n_in], refs[n_in + n_jobs])
        for s_ref, d_ref in zip(refs[n_in:n_in + n_jobs], refs[n_in + n_jobs + 1:]):
            d_ref[...] = s_ref[...].astype(BF16)

    outs = pl.pallas_call(
        wrapped,
        grid=grid,
        in_specs=list(in_specs) + [s[0] for s in specs],
        out_specs=[out_spec] + [s[1] for s in specs],
        out_shape=[out_shape] + [s[2] for s in specs],
        compiler_params=_params(("arbitrary", "arbitrary"), vmem_mib),
        name=name,
    )(*args, *[j.src for j in jobs])
    return outs[0], list(outs[1:])


def _cast_body(s_ref, d_ref):
    d_ref[...] = s_ref[...].astype(BF16)


def _cast_now(job):
    src_spec, dst_spec, shape = _cast_specs(job, 1)
    return pl.pallas_call(
        _cast_body,
        grid=(job.nblk, 1),
        in_specs=[src_spec],
        out_specs=dst_spec,
        out_shape=shape,
        compiler_params=_params(("arbitrary", "arbitrary"), 32),
        name="cast_bf16",
    )(job.src)


def _ada_down_body(cv_ref, down_ref, o_ref):
    cv = cv_ref[...]
    s = (cv * jax.nn.sigmoid(cv)).astype(BF16)
    o_ref[...] = _dot(s, down_ref[...].astype(BF16))


def _ada_up_body(t_ref, up_ref, b_ref, o_ref):
    o_ref[...] = _dot(t_ref[...].astype(BF16), up_ref[...].astype(BF16)) + b_ref[...]


def _ada_mod(cv, ada_down, ada_up, ada_b):
    t = pl.pallas_call(
        _ada_down_body,
        grid=(DEPTH,),
        in_specs=[pl.BlockSpec((8, D_MODEL), lambda l: (0, 0)),
                  pl.BlockSpec((None, D_MODEL, ADA_RANK), lambda l: (l, 0, 0))],
        out_specs=pl.BlockSpec((None, 8, ADA_RANK), lambda l: (l, 0, 0)),
        out_shape=jax.ShapeDtypeStruct((DEPTH, 8, ADA_RANK), F32),
        compiler_params=_params(("arbitrary",), 32),
        name="ada_down",
    )(cv, ada_down)
    bn = 4608
    nj = N_MOD * D_MODEL // bn
    return pl.pallas_call(
        _ada_up_body,
        grid=(DEPTH, nj),
        in_specs=[pl.BlockSpec((None, 8, ADA_RANK), lambda l, j: (l, 0, 0)),
                  pl.BlockSpec((None, ADA_RANK, bn), lambda l, j: (l, 0, j)),
                  pl.BlockSpec((None, 1, bn), lambda l, j: (l, 0, j))],
        out_specs=pl.BlockSpec((None, 8, bn), lambda l, j: (l, 0, j)),
        out_shape=jax.ShapeDtypeStruct((DEPTH, 8, N_MOD * D_MODEL), F32),
        compiler_params=_params(("arbitrary", "arbitrary"), 32),
        name="ada_up",
    )(t, ada_up, ada_b.reshape(DEPTH, 1, N_MOD * D_MODEL))


def _norm_mod_body(h_ref, g_ref, shift_ref, scale_ref, o_ref):
    bm = h_ref.shape[0]
    in_last = pl.program_id(0) == pl.num_programs(0) - 1
    gains = [g_ref[...] * (1.0 + scale_ref[k:k + 1, :]) for k in range(2)]
    shifts = [shift_ref[k:k + 1, :] for k in range(2)]
    for r0 in range(0, bm, NORM_ROWS):
        if r0 >= SEQ % bm:
            gain = jnp.where(in_last, gains[1], gains[0])
            shift = jnp.where(in_last, shifts[1], shifts[0])
        else:
            gain, shift = gains[0], shifts[0]
        x = h_ref[r0:r0 + NORM_ROWS, :]
        r = lax.rsqrt(jnp.mean(x * x, axis=-1, keepdims=True) + NORM_EPS)
        o_ref[r0:r0 + NORM_ROWS, :] = ((x * r) * gain + shift).astype(o_ref.dtype)


def _norm_mod(h, g_row, m, i_mod):
    return pl.pallas_call(
        _norm_mod_body,
        grid=(S_ALL // BM,),
        in_specs=[pl.BlockSpec((BM, D_MODEL), lambda i: (i, 0)),
                  pl.BlockSpec((1, D_MODEL), lambda i: (0, 0)),
                  pl.BlockSpec((8, D_MODEL), lambda i: (0, 3 * i_mod)),
                  pl.BlockSpec((8, D_MODEL), lambda i: (0, 3 * i_mod + 1))],
        out_specs=pl.BlockSpec((BM, D_MODEL), lambda i: (i, 0)),
        out_shape=jax.ShapeDtypeStruct((S_ALL, D_MODEL), BF16),
        compiler_params=_params(("arbitrary",), 48),
        name="norm_mod",
    )(h, g_row, m, m)


def _col_chunks(ref):
    return [slice(c, c + MXU_N) for c in range(0, ref.shape[1], MXU_N)]


def _swiglu_body(z_ref, wa_ref, wb_ref, o_ref):
    z = z_ref[...]
    for sl in _col_chunks(o_ref):
        a = _dot(z, wa_ref[:, sl])
        b = _dot(z, wb_ref[:, sl])
        o_ref[:, sl] = (a * _sigmoid(a) * b).astype(o_ref.dtype)


def _mm_swiglu(z, wi, jobs=()):
    bn = 512
    nj = D_FF // bn
    return _host_call(
        _swiglu_body, (S_ALL // BM, nj),
        [pl.BlockSpec((BM, D_MODEL), lambda i, j: (i, 0)),
         pl.BlockSpec((D_MODEL, bn), lambda i, j: (0, j)),
         pl.BlockSpec((D_MODEL, bn), lambda i, j: (0, j + nj))],
        pl.BlockSpec((BM, bn), lambda i, j: (i, j)),
        jax.ShapeDtypeStruct((S_ALL, D_FF), BF16),
        (z, wi, wi), jobs, 56, "mm_swiglu")


def _resid_body(x_ref, w_ref, h_ref, gate_ref, o_ref, *, coef):
    x = x_ref[...]
    bm = x_ref.shape[0]
    is_ctx = pl.program_id(0) * bm + lax.broadcasted_iota(jnp.int32, (bm, 1), 0) >= SEQ
    for sl in _col_chunks(o_ref):
        gate = jnp.where(is_ctx, gate_ref[1:2, sl], gate_ref[0:1, sl])
        o_ref[:, sl] = h_ref[:, sl] + (coef * gate) * _dot(x, w_ref[:, sl])


def _mm_resid(xin, w, h, m, i_mod, coef, jobs=(), latent_only=False):
    k = xin.shape[1]
    rows, bm = (SEQ, BM_LATENT) if latent_only else (S_ALL, BM_WIDE)
    bn = 512
    nj = D_MODEL // bn
    return _host_call(
        functools.partial(_resid_body, coef=coef), (rows // bm, nj),
        [pl.BlockSpec((bm, k), lambda i, j: (i, 0)),
         pl.BlockSpec((k, bn), lambda i, j: (0, j)),
         pl.BlockSpec((bm, bn), lambda i, j: (i, j)),
         pl.BlockSpec((8, bn), lambda i, j: (0, (3 * i_mod + 2) * nj + j))],
        pl.BlockSpec((bm, bn), lambda i, j: (i, j)),
        jax.ShapeDtypeStruct((rows, D_MODEL), F32),
        (xin, w, h, m), jobs, 56, "mm_resid")


def _plain_body(z_ref, w_ref, o_ref):
    z = z_ref[...]
    for sl in _col_chunks(o_ref):
        o_ref[:, sl] = _dot(z, w_ref[:, sl]).astype(o_ref.dtype)


def _mm_cols(z, w, col0, ncols, jobs=()):
    bm, bn = BM_WIDE, 512
    j0 = col0 // bn
    return _host_call(
        _plain_body, (S_ALL // bm, ncols // bn),
        [pl.BlockSpec((bm, D_MODEL), lambda i, j: (i, 0)),
         pl.BlockSpec((D_MODEL, bn), lambda i, j: (0, j0 + j))],
        pl.BlockSpec((bm, bn), lambda i, j: (i, j)),
        jax.ShapeDtypeStruct((S_ALL, ncols), BF16),
        (z, w), jobs, 56, "mm_cols")


def _merge_body(z_ref, y0, y1, y2, y3, g0, g1, g2, g3, b0, b1, b2, b3, o_ref):
    z = z_ref[...]
    acc = None
    for y_ref, g_ref, b_ref in ((y0, g0, b0), (y1, g1, b1), (y2, g2, b2), (y3, g3, b3)):
        t = _sigmoid(_dot(z, g_ref[...])) * _dot(y_ref[...], b_ref[...])
        acc = t if acc is None else acc + t
    o_ref[...] = acc.astype(o_ref.dtype)


def _merge(z, ys, w_gates, w_branch, jobs=()):
    bn = 256
    nj = D_MODEL // bn
    z_spec = pl.BlockSpec((BM, D_MODEL), lambda i, j: (i, 0))
    y_spec = pl.BlockSpec((BM, BRANCH_W), lambda i, j: (i, 0))
    g_specs = [pl.BlockSpec((D_MODEL, bn), functools.partial(lambda i, j, r: (0, r * nj + j), r=r))
               for r in range(N_BRANCH)]
    b_specs = [pl.BlockSpec((BRANCH_W, bn), functools.partial(lambda i, j, r: (r, j), r=r))
               for r in range(N_BRANCH)]
    return _host_call(
        _merge_body, (S_ALL // BM, nj),
        [z_spec] + [y_spec] * N_BRANCH + g_specs + b_specs,
        pl.BlockSpec((BM, bn), lambda i, j: (i, j)),
        jax.ShapeDtypeStruct((S_ALL, D_MODEL), BF16),
        (z, *ys, *([w_gates] * N_BRANCH), *([w_branch] * N_BRANCH)), jobs, 60, "merge")


def _head_norm_rope(t, g, cos, sin_lo, sin_hi):
    t = t * lax.rsqrt(jnp.mean(t * t, axis=-1, keepdims=True) + NORM_EPS) * g
    return t * cos + pltpu.roll(t, 96, 1) * sin_lo + pltpu.roll(t, 32, 1) * sin_hi


def _kprep_body(k_ref, g_ref, cos_ref, lo_ref, hi_ref, o_ref):
    for hk in range(N_KV_HEADS):
        sl = slice(hk * HEAD_DIM, (hk + 1) * HEAD_DIM)
        t = _head_norm_rope(k_ref[:, sl].astype(F32), g_ref[...], cos_ref[...], lo_ref[...], hi_ref[...])
        o_ref[:, sl] = t.astype(o_ref.dtype)


def _kprep(p, k_g, cos, sin_lo, sin_hi):
    tab = pl.BlockSpec((BS, HEAD_DIM), lambda i: (i, 0))
    return pl.pallas_call(
        _kprep_body,
        grid=(S_ALL // BS,),
        in_specs=[pl.BlockSpec((BS, KV_W), lambda i: (i, COL_K // KV_W)),
                  pl.BlockSpec((1, HEAD_DIM), lambda i: (0, 0)), tab, tab, tab],
        out_specs=pl.BlockSpec((BS, KV_W), lambda i: (i, 0)),
        out_shape=jax.ShapeDtypeStruct((S_ALL, KV_W), BF16),
        compiler_params=_params(("arbitrary",), 32),
        name="kprep",
    )(p, k_g, cos, sin_lo, sin_hi)


def _attn_body(sink_ref, q_ref, g_ref, cos_ref, lo_ref, hi_ref,
               kp_ref, kc_ref, kn_ref, kx_ref, vp_ref, vc_ref, vn_ref, vx_ref, o_ref):
    n = pl.program_id(0)
    nb_lat = SEQ // ATTN_BLOCK
    rows = GQA_GROUP * ATTN_BLOCK
    n_loc = 3 * ATTN_BLOCK
    is_lat = n < nb_lat
    lo = jnp.where(is_lat, jnp.where(n >= 1, 0, ATTN_BLOCK), n_loc)
    hi = jnp.where(n < nb_lat - 1, n_loc, 2 * ATTN_BLOCK)
    r = lax.broadcasted_iota(jnp.int32, (rows, n_loc), 0) & (ATTN_BLOCK - 1)
    a = lax.broadcasted_iota(jnp.int32, (rows, n_loc), 1)
    valid = (a >= r) & (a <= r + 2 * WINDOW) & (a >= lo) & (a < hi)
    row_head = lax.broadcasted_iota(jnp.int32, (rows, 1), 0) // ATTN_BLOCK
    scale = HEAD_DIM ** -0.5
    for hk in range(N_KV_HEADS):
        sl = slice(hk * HEAD_DIM, (hk + 1) * HEAD_DIM)
        qs = []
        for g in range(GQA_GROUP):
            h = hk * GQA_GROUP + g
            t = q_ref[:, h * HEAD_DIM:(h + 1) * HEAD_DIM].astype(F32)
            qs.append(_head_norm_rope(t, g_ref[...], cos_ref[...], lo_ref[...], hi_ref[...]).astype(BF16))
        q = jnp.concatenate(qs, axis=0)
        k_loc = jnp.concatenate([kp_ref[:, sl], kc_ref[:, sl], kn_ref[:, sl]], axis=0)
        v_loc = jnp.concatenate([vp_ref[:, sl], vc_ref[:, sl], vn_ref[:, sl]], axis=0)
        s_loc = jnp.where(valid, _dot_nt(q, k_loc) * scale, MASK_VALUE)
        s_ctx = _dot_nt(q, kx_ref[:, sl]) * scale
        snk = jnp.zeros((rows, 1), F32)
        for g in range(GQA_GROUP):
            snk = jnp.where(row_head == g, sink_ref[hk * GQA_GROUP + g], snk)
        m = jnp.maximum(jnp.maximum(jnp.max(s_loc, axis=-1, keepdims=True),
                                    jnp.max(s_ctx, axis=-1, keepdims=True)), snk)
        p_loc = jnp.exp(s_loc - m)
        p_ctx = jnp.exp(s_ctx - m)
        den = (jnp.sum(p_loc, axis=-1, keepdims=True) + jnp.sum(p_ctx, axis=-1, keepdims=True)
               + jnp.exp(snk - m))
        o = (_dot(p_loc.astype(BF16), v_loc) + _dot(p_ctx.astype(BF16), vx_ref[:, sl])) / den
        for g in range(GQA_GROUP):
            h = hk * GQA_GROUP + g
            o_ref[:, h * HEAD_DIM:(h + 1) * HEAD_DIM] = o[g * ATTN_BLOCK:(g + 1) * ATTN_BLOCK].astype(o_ref.dtype)


def _attention(p, k_rot, q_g, sink, cos, sin_lo, sin_hi):
    nb = S_ALL // ATTN_BLOCK
    blk_ctx = SEQ // CTX_LEN
    tab = pl.BlockSpec((ATTN_BLOCK, HEAD_DIM), lambda n: (n, 0))
    prev = lambda n: jnp.maximum(n - 1, 0)
    nxt = lambda n: jnp.minimum(n + 1, nb - 1)
    jv = COL_V // KV_W
    return pl.pallas_call(
        _attn_body,
        grid=(nb,),
        in_specs=[pl.BlockSpec(memory_space=pltpu.SMEM),
                  pl.BlockSpec((ATTN_BLOCK, Q_W), lambda n: (n, 0)),
                  pl.BlockSpec((1, HEAD_DIM), lambda n: (0, 0)), tab, tab, tab,
                  pl.BlockSpec((ATTN_BLOCK, KV_W), lambda n: (prev(n), 0)),
                  pl.BlockSpec((ATTN_BLOCK, KV_W), lambda n: (n, 0)),
                  pl.BlockSpec((ATTN_BLOCK, KV_W), lambda n: (nxt(n), 0)),
                  pl.BlockSpec((CTX_LEN, KV_W), lambda n: (blk_ctx, 0)),
                  pl.BlockSpec((ATTN_BLOCK, KV_W), lambda n: (prev(n), jv)),
                  pl.BlockSpec((ATTN_BLOCK, KV_W), lambda n: (n, jv)),
                  pl.BlockSpec((ATTN_BLOCK, KV_W), lambda n: (nxt(n), jv)),
                  pl.BlockSpec((CTX_LEN, KV_W), lambda n: (blk_ctx, jv))],
        out_specs=pl.BlockSpec((ATTN_BLOCK, Q_W), lambda n: (n, 0)),
        out_shape=jax.ShapeDtypeStruct((S_ALL, Q_W), BF16),
        compiler_params=_params(("arbitrary",), 32),
        name="attention",
    )(sink, p, q_g, cos, sin_lo, sin_hi, k_rot, k_rot, k_rot, k_rot, p, p, p, p)


def _sgu_body(u_ref, v_ref, lg_ref, lb_ref, ws_ref, bs_ref, o_ref):
    half = u_ref.shape[1] // SGU_CH
    for g in range(half):
        sl = slice(g * SGU_CH, (g + 1) * SGU_CH)
        vg = jax.nn.gelu(v_ref[:, sl].astype(F32))
        mu = jnp.mean(vg, axis=-1, keepdims=True)
        d = vg - mu
        var = jnp.mean(d * d, axis=-1, keepdims=True)
        vn = (d * lax.rsqrt(var + NORM_EPS) * lg_ref[:, sl] + lb_ref[:, sl]).astype(BF16)
        for c in range(u_ref.shape[0] // CHUNK):
            rs = slice(c * CHUNK, (c + 1) * CHUNK)
            mixed = _dot(ws_ref[g], vn[rs]) + bs_ref[:, g:g + 1]
            o_ref[rs, sl] = (jax.nn.gelu(u_ref[rs, sl].astype(F32)) * mixed).astype(o_ref.dtype)


def _sgu(p, ln_g, ln_b, w_s, b_s_t):
    bw = 512
    gh = bw // SGU_CH
    ju, jv = COL_SGU_U // bw, COL_SGU_V // bw
    return pl.pallas_call(
        _sgu_body,
        grid=(S_ALL // BS, SGU_W // bw),
        in_specs=[pl.BlockSpec((BS, bw), lambda i, c: (i, ju + c)),
                  pl.BlockSpec((BS, bw), lambda i, c: (i, jv + c)),
                  pl.BlockSpec((1, bw), lambda i, c: (0, c)),
                  pl.BlockSpec((1, bw), lambda i, c: (0, c)),
                  pl.BlockSpec((gh, CHUNK, CHUNK), lambda i, c: (c, 0, 0)),
                  pl.BlockSpec((None, CHUNK, gh), lambda i, c: (c, 0, 0))],
        out_specs=pl.BlockSpec((BS, bw), lambda i, c: (i, c)),
        out_shape=jax.ShapeDtypeStruct((S_ALL, SGU_W), BF16),
        compiler_params=_params(("arbitrary", "arbitrary"), 32),
        name="sgu",
    )(p, p, ln_g, ln_b, w_s, b_s_t)


HALO = 16


def _conv_body(x_ref, b_ref, c_ref, xp_ref, cp_ref, xn_ref, cn_ref, w_ref, o_ref):
    bs = x_ref.shape[0]
    i = pl.program_id(0)
    in_last = i == pl.num_programs(0) - 1
    z = c_ref[...].astype(F32) * x_ref[...].astype(F32)
    zp = cp_ref[HALO - 1:HALO, :].astype(F32) * xp_ref[HALO - 1:HALO, :].astype(F32)
    zn = cn_ref[0:1, :].astype(F32) * xn_ref[0:1, :].astype(F32)
    zp = jnp.where(i != 0, zp, 0.0)
    zn = jnp.where(in_last, 0.0, zn)
    row = lax.broadcasted_iota(jnp.int32, (bs, 1), 0)
    z_prev = jnp.where(row == 0, zp, pltpu.roll(z, 1, 0))
    z_next = jnp.where(row == bs - 1, zn, pltpu.roll(z, bs - 1, 0))
    z_prev = jnp.where(in_last & (row == SEQ % bs), 0.0, z_prev)
    z_next = jnp.where(in_last & (row == SEQ % bs - 1), 0.0, z_next)
    y = w_ref[0:1, :] * z_prev + w_ref[1:2, :] * z + w_ref[2:3, :] * z_next
    o_ref[...] = (b_ref[...].astype(F32) * y).astype(o_ref.dtype)


def _short_conv(p, conv_w):
    bw = 512
    jx, jb, jc = COL_CONV_X // bw, COL_CONV_B // bw, COL_CONV_C // bw
    rb = BS // HALO
    nh = S_ALL // HALO
    prev = lambda i: jnp.maximum(i * rb - 1, 0)
    nxt = lambda i: jnp.minimum((i + 1) * rb, nh - 1)
    return pl.pallas_call(
        _conv_body,
        grid=(S_ALL // BS, CONV_W // bw),
        in_specs=[pl.BlockSpec((BS, bw), lambda i, c: (i, jx + c)),
                  pl.BlockSpec((BS, bw), lambda i, c: (i, jb + c)),
                  pl.BlockSpec((BS, bw), lambda i, c: (i, jc + c)),
                  pl.BlockSpec((HALO, bw), lambda i, c: (prev(i), jx + c)),
                  pl.BlockSpec((HALO, bw), lambda i, c: (prev(i), jc + c)),
                  pl.BlockSpec((HALO, bw), lambda i, c: (nxt(i), jx + c)),
                  pl.BlockSpec((HALO, bw), lambda i, c: (nxt(i), jc + c)),
                  pl.BlockSpec((3, bw), lambda i, c: (0, c))],
        out_specs=pl.BlockSpec((BS, bw), lambda i, c: (i, c)),
        out_shape=jax.ShapeDtypeStruct((S_ALL, CONV_W), BF16),
        compiler_params=_params(("arbitrary", "arbitrary"), 32),
        name="short_conv",
    )(p, p, p, p, p, p, p, conv_w)


def _dft_consts():
    def cs(n, rows, cols):
        ang = 2.0 * np.pi * ((np.outer(rows, cols)) % n) / n
        return np.cos(ang), np.sin(ang)

    t1, t2 = DFT_T1, DFT_T2
    c1, s1 = cs(t1, np.arange(t1), np.arange(t1))
    m1 = np.concatenate([c1, -s1], axis=0)
    cw, sw = cs(SEQ, np.arange(t1), np.arange(t2))
    c2, s2 = cs(t2, np.arange(t2), np.arange(t2))
    m2 = np.block([[c2, s2], [-s2, c2]])
    cx, sx = cs(CTX_LEN, np.arange(CTX_LEN), np.arange(CTX_LEN))
    m2x = np.block([[cx, sx], [-sx, cx]])
    cc, sc = cs(FOURIER_CH, np.arange(FOURIER_CH), np.arange(FOURIER_CH))
    nl = 1.0 / np.sqrt(SEQ * FOURIER_CH)
    nx = 1.0 / np.sqrt(CTX_LEN * FOURIER_CH)
    as_bf = lambda a: jnp.asarray(a, dtype=F32).astype(BF16)
    return dict(
        m1=as_bf(m1), m2=as_bf(m2), m2x=as_bf(m2x),
        cw=jnp.asarray(cw[:, :, None], F32), sw=jnp.asarray(sw[:, :, None], F32),
        cwx=jnp.ones((1, CTX_LEN, 1), F32), swx=jnp.zeros((1, CTX_LEN, 1), F32),
        cc_l=as_bf(cc * nl), sc_l=as_bf(sc * nl), cc_x=as_bf(cc * nx), sc_x=as_bf(sc * nx))


def _dft1_body(m_ref, x_ref, o_ref):
    o_ref[...] = _dot(m_ref[...], x_ref[...]).astype(o_ref.dtype)


def _dft_stage1(f, m1):
    ncol = DFT_T2 * FOURIER_W
    bn = 8192
    fv = f.reshape(S_ALL // DFT_T2, ncol)
    return pl.pallas_call(
        _dft1_body,
        grid=(ncol // bn,),
        in_specs=[pl.BlockSpec((2 * DFT_T1, DFT_T1), lambda j: (0, 0)),
                  pl.BlockSpec((DFT_T1, bn), lambda j: (0, j))],
        out_specs=pl.BlockSpec((2 * DFT_T1, bn), lambda j: (0, j)),
        out_shape=jax.ShapeDtypeStruct((2 * DFT_T1, ncol), BF16),
        compiler_params=_params(("arbitrary",), 32),
        name="dft_stage1",
    )(m1, fv)


def _dft2_body(ar_ref, ai_ref, cw_ref, sw_ref, m2_ref, cc_ref, sc_ref, o_ref):
    n_grp, t2 = ar_ref.shape[0], ar_ref.shape[1]
    for k in range(n_grp):
        ar = ar_ref[k].astype(F32)
        ai = ai_ref[k].astype(F32)
        cw = cw_ref[k]
        sw = sw_ref[k]
        b = jnp.concatenate([ar * cw + ai * sw, ai * cw - ar * sw], axis=0).astype(BF16)
        p = _dot(m2_ref[...], b)
        pr = p[:t2].astype(BF16)
        pi = p[t2:].astype(BF16)
        for g in range(FOURIER_GROUPS):
            sl = slice(g * FOURIER_CH, (g + 1) * FOURIER_CH)
            y = _dot(pr[:, sl], cc_ref[...]) + _dot(pi[:, sl], sc_ref[...])
            o_ref[:, k * FOURIER_W + g * FOURIER_CH:k * FOURIER_W + (g + 1) * FOURIER_CH] = y.astype(o_ref.dtype)


def _dft_stage2(a3, n_ka, t2, im_off, cw, sw, m2, cc, sc):
    grp = min(n_ka, 4)
    io = im_off // grp
    return pl.pallas_call(
        _dft2_body,
        grid=(n_ka // grp,),
        in_specs=[pl.BlockSpec((grp, t2, FOURIER_W), lambda ka: (ka, 0, 0)),
                  pl.BlockSpec((grp, t2, FOURIER_W), lambda ka: (io + ka, 0, 0)),
                  pl.BlockSpec((grp, t2, 1), lambda ka: (ka, 0, 0)),
                  pl.BlockSpec((grp, t2, 1), lambda ka: (ka, 0, 0)),
                  pl.BlockSpec((2 * t2, 2 * t2), lambda ka: (0, 0)),
                  pl.BlockSpec((FOURIER_CH, FOURIER_CH), lambda ka: (0, 0)),
                  pl.BlockSpec((FOURIER_CH, FOURIER_CH), lambda ka: (0, 0))],
        out_specs=pl.BlockSpec((t2, grp * FOURIER_W), lambda ka: (0, ka)),
        out_shape=jax.ShapeDtypeStruct((t2, n_ka * FOURIER_W), BF16),
        compiler_params=_params(("arbitrary",), 32),
        name="dft_stage2",
    )(a3, a3, cw, sw, m2, cc, sc)


def _fourier(f, k):
    a = _dft_stage1(f, k["m1"]).reshape(2 * DFT_T1, DFT_T2, FOURIER_W)
    y_lat = _dft_stage2(a, DFT_T1, DFT_T2, DFT_T1, k["cw"], k["sw"], k["m2"], k["cc_l"], k["sc_l"])
    y_lat = y_lat.reshape(SEQ, FOURIER_W)
    fx = jnp.stack([f[SEQ:], jnp.zeros((CTX_LEN, FOURIER_W), f.dtype)])
    y_ctx = _dft_stage2(fx, 1, CTX_LEN, 1, k["cwx"], k["swx"], k["m2x"], k["cc_x"], k["sc_x"])
    return jnp.concatenate([y_lat, y_ctx], axis=0)


def _rope_tables():
    pos = np.arange(SEQ)
    ax = HEAD_DIM // 2
    inv_freq = 1.0 / (ROPE_THETA ** (np.arange(0, ax, 2, dtype=np.float32) / ax)).astype(np.float32)
    ang_r = (pos // GRID_W).astype(np.float32)[:, None] * inv_freq[None, :]
    ang_c = (pos % GRID_W).astype(np.float32)[:, None] * inv_freq[None, :]
    emb = np.concatenate([ang_r, ang_r, ang_c, ang_c], axis=-1).astype(np.float32)
    cos = np.cos(emb.astype(np.float64))
    sin = np.sin(emb.astype(np.float64))
    first = (np.arange(HEAD_DIM) % (HEAD_DIM // 2)) < HEAD_DIM // 4
    sin_lo = np.where(first[None, :], -sin, 0.0)
    sin_hi = np.where(first[None, :], 0.0, sin)
    pad = lambda a, v: np.concatenate([a, np.full((CTX_LEN, HEAD_DIM), v)], axis=0)
    return (jnp.asarray(pad(cos, 1.0), F32), jnp.asarray(pad(sin_lo, 0.0), F32),
            jnp.asarray(pad(sin_hi, 0.0), F32))


def kernel(x, c, ctx, c_ctx, ada_down, ada_up, ada_b, norm_g, ffn_wi, ffn_wo, w_in, q_norm, k_norm,
           sink, sgu_ln_g, sgu_ln_b, sgu_w, sgu_b, conv_w, w_branch, w_out):
    assert x.shape == (1, SEQ, D_MODEL) and ctx.shape == (1, CTX_LEN, D_MODEL)
    h = jnp.concatenate([x[0], ctx[0]], axis=0)
    cv = jnp.concatenate([c, c_ctx[None, :], jnp.zeros((6, D_MODEL), F32)], axis=0)
    m_all = _ada_mod(cv, ada_down, ada_up, ada_b)
    cos, sin_lo, sin_hi = _rope_tables()
    dft = _dft_consts()

    wb4 = w_branch.reshape(DEPTH, N_BRANCH * BRANCH_W, D_MODEL)

    def job(l, name):
        big = (512, 1024)
        gate_rb = 2048 if l == 0 else 1024
        return dict(
            wi0=lambda: _CastJob(ffn_wi, (l, 0), D_MODEL, 0, 2 * D_FF, *big),
            wi1=lambda: _CastJob(ffn_wi, (l, 1), D_MODEL, 0, 2 * D_FF, *big),
            wo0=lambda: _CastJob(ffn_wo, (l, 0), D_FF, 0, D_MODEL, *big),
            wo1=lambda: _CastJob(ffn_wo, (l, 1), D_FF, 0, D_MODEL, *big),
            wic=lambda: _CastJob(w_in, (l,), D_MODEL, 0, COL_GATES, 1024, 512),
            wig=lambda: _CastJob(w_in, (l,), D_MODEL, COL_GATES, N_BRANCH * D_MODEL, gate_rb, 512),
            wb=lambda: _CastJob(wb4, (l,), N_BRANCH * BRANCH_W, 0, D_MODEL, *big),
            wout=lambda: _CastJob(w_out, (l,), D_MODEL, 0, D_MODEL, *big))[name]()

    def rides(call, l):
        nxt = {"swiglu1": ["wi0"], "resid1": ["wo0"], "cols": ["wic"], "merge": ["wig"],
               "resid_out": ["wout"], "swiglu2": ["wi1", "wb"], "resid2": ["wo1"]}[call]
        keys = [(l + 1, n) for n in nxt] if l + 1 < DEPTH else []
        if l == 0:
            keys += [(0, n) for n in {"swiglu1": ["wig"], "resid1": ["wout"], "cols": ["wb", "wi1"],
                                      "swiglu2": ["wo1"]}.get(call, [])]
        return keys

    wts = {(0, n): _cast_now(job(0, n)) for n in ("wi0", "wo0", "wic")}
    for l in range(DEPTH):
        m = m_all[l]

        def run(call, fn):
            keys = rides(call, l)
            out, cast = fn([job(*k) for k in keys])
            wts.update(zip(keys, cast))
            return out

        z = _norm_mod(h, norm_g[l, 0][None, :], m, 0)
        g = run("swiglu1", lambda jb: _mm_swiglu(z, wts[l, "wi0"], jb))
        h = run("resid1", lambda jb: _mm_resid(g, wts[l, "wo0"], h, m, 0, 0.5, jb))
        z = _norm_mod(h, norm_g[l, 1][None, :], m, 1)
        p = run("cols", lambda jb: _mm_cols(z, wts[l, "wic"], 0, COL_FOURIER, jb))
        f, _ = _mm_cols(z, wts[l, "wic"], COL_FOURIER, FOURIER_W)
        k_rot = _kprep(p, k_norm[l][None, :], cos, sin_lo, sin_hi)
        ys = (_attention(p, k_rot, q_norm[l][None, :], sink[l], cos, sin_lo, sin_hi),
              _sgu(p, sgu_ln_g[l][None, :], sgu_ln_b[l][None, :], sgu_w[l].astype(BF16),
                   sgu_b[l].reshape(SGU_W // 512, 512 // SGU_CH, CHUNK).transpose(0, 2, 1)),
              _short_conv(p, conv_w[l]),
              _fourier(f, dft))
        merged = run("merge", lambda jb: _merge(z, ys, wts[l, "wig"], wts[l, "wb"], jb))
        h = run("resid_out", lambda jb: _mm_resid(merged, wts[l, "wout"], h, m, 1, 1.0, jb))
        z = _norm_mod(h, norm_g[l, 2][None, :], m, 2)
        g = run("swiglu2", lambda jb: _mm_swiglu(z, wts[l, "wi1"], jb))
        h = run("resid2", lambda jb: _mm_resid(g, wts[l, "wo1"], h, m, 2, 0.5, jb,
                                               latent_only=l == DEPTH - 1))
    return h[None]
```

```python
import functools
from typing import NamedTuple

import numpy as np
import jax
import jax.numpy as jnp
from jax import lax
from jax.experimental import pallas as pl
from jax.experimental.pallas import tpu as pltpu

D_MODEL = 4096
SEQ = 8192
DEPTH = 4
GRID_W = 64
CTX_LEN = 256
S_ALL = SEQ + CTX_LEN
HEAD_DIM = 128
N_Q_HEADS = 8
N_KV_HEADS = 2
GQA_GROUP = N_Q_HEADS // N_KV_HEADS
Q_W = N_Q_HEADS * HEAD_DIM
KV_W = N_KV_HEADS * HEAD_DIM
WINDOW = 128
ATTN_BLOCK = 128
ROPE_THETA = 10000.0
SGU_GROUPS = 8
SGU_CH = 128
SGU_W = SGU_GROUPS * SGU_CH
CHUNK = 128
CONV_W = 1024
FOURIER_GROUPS = 4
FOURIER_CH = 256
FOURIER_W = FOURIER_GROUPS * FOURIER_CH
N_BRANCH = 4
BRANCH_W = 1024
D_FF = 3584
ADA_RANK = 256
N_MOD = 9
NORM_EPS = 1e-6
MASK_VALUE = -1e30

COL_Q = 0
COL_K = Q_W
COL_V = COL_K + KV_W
COL_SGU_U = COL_V + KV_W
COL_SGU_V = COL_SGU_U + SGU_W
COL_CONV_X = COL_SGU_V + SGU_W
COL_CONV_B = COL_CONV_X + CONV_W
COL_CONV_C = COL_CONV_B + CONV_W
COL_FOURIER = COL_CONV_C + CONV_W
COL_GATES = COL_FOURIER + FOURIER_W
IN_COLS = COL_GATES + N_BRANCH * D_MODEL

BM = 768
BM_WIDE = 1408
BM_LATENT = 2048
BS = 768
NORM_ROWS = 16
DFT_T1 = 64
DFT_T2 = 128
MXU_N = 256
MIB = 1024 * 1024

F32 = jnp.float32
BF16 = jnp.bfloat16


def _params(sem, vmem_mib):
    return pltpu.CompilerParams(dimension_semantics=sem, vmem_limit_bytes=vmem_mib * MIB)


def _dot(a, b):
    return jnp.dot(a, b, preferred_element_type=F32)


def _dot_nt(a, b):
    return lax.dot_general(a, b, (((1,), (1,)), ((), ())), preferred_element_type=F32)


def _sigmoid(x):
    return 0.5 * jnp.tanh(0.5 * x) + 0.5


class _CastJob(NamedTuple):
    src: jax.Array
    lead: tuple
    rows: int
    col0: int
    ncols: int
    rb: int
    cb: int

    @property
    def nblk(self):
        return (self.rows // self.rb) * (self.ncols // self.cb)


def _cast_specs(job, nj):
    nbc = job.ncols // job.cb
    c0 = job.col0 // job.cb
    last = job.nblk - 1

    def src_map(i, j):
        tb = jnp.minimum(i * nj + j, last)
        return job.lead + (tb // nbc, c0 + tb % nbc)

    def dst_map(i, j):
        tb = jnp.minimum(i * nj + j, last)
        return (tb // nbc, tb % nbc)

    return (pl.BlockSpec((None,) * len(job.lead) + (job.rb, job.cb), src_map),
            pl.BlockSpec((job.rb, job.cb), dst_map),
            jax.ShapeDtypeStruct((job.rows, job.ncols), BF16))


def _host_call(body, grid, in_specs, out_spec, out_shape, args, jobs, vmem_mib, name):
    n_in, n_jobs = len(in_specs), len(jobs)
    assert all(j.nblk <= grid[0] * grid[1] for j in jobs)
    specs = [_cast_specs(j, grid[1]) for j in jobs]

    def wrapped(*refs):
        body(*refs[:n_in], refs[n_in + n_jobs])
        for s_ref, d_ref in zip(refs[n_in:n_in + n_jobs], refs[n_in + n_jobs + 1:]):
            d_ref[...] = s_ref[...].astype(BF16)

    outs = pl.pallas_call(
        wrapped,
        grid=grid,
        in_specs=list(in_specs) + [s[0] for s in specs],
        out_specs=[out_spec] + [s[1] for s in specs],
        out_shape=[out_shape] + [s[2] for s in specs],
        compiler_params=_params(("arbitrary", "arbitrary"), vmem_mib),
        name=name,
    )(*args, *[j.src for j in jobs])
    return outs[0], list(outs[1:])


def _cast_body(s_ref, d_ref):
    d_ref[...] = s_ref[...].astype(BF16)


def _cast_now(job):
    src_spec, dst_spec, shape = _cast_specs(job, 1)
    return pl.pallas_call(
        _cast_body,
        grid=(job.nblk, 1),
        in_specs=[src_spec],
        out_specs=dst_spec,
        out_shape=shape,
        compiler_params=_params(("arbitrary", "arbitrary"), 32),
        name="cast_bf16",
    )(job.src)


def _ada_down_body(cv_ref, down_ref, o_ref):
    cv = cv_ref[...]
    s = (cv * jax.nn.sigmoid(cv)).astype(BF16)
    o_ref[...] = _dot(s, down_ref[...].astype(BF16))


def _ada_up_body(t_ref, up_ref, b_ref, o_ref):
    o_ref[...] = _dot(t_ref[...].astype(BF16), up_ref[...].astype(BF16)) + b_ref[...]


def _ada_mod(cv, ada_down, ada_up, ada_b):
    t = pl.pallas_call(
        _ada_down_body,
        grid=(DEPTH,),
        in_specs=[pl.BlockSpec((8, D_MODEL), lambda l: (0, 0)),
                  pl.BlockSpec((None, D_MODEL, ADA_RANK), lambda l: (l, 0, 0))],
        out_specs=pl.BlockSpec((None, 8, ADA_RANK), lambda l: (l, 0, 0)),
        out_shape=jax.ShapeDtypeStruct((DEPTH, 8, ADA_RANK), F32),
        compiler_params=_params(("arbitrary",), 32),
        name="ada_down",
    )(cv, ada_down)
    bn = 4608
    nj = N_MOD * D_MODEL // bn
    return pl.pallas_call(
        _ada_up_body,
        grid=(DEPTH, nj),
        in_specs=[pl.BlockSpec((None, 8, ADA_RANK), lambda l, j: (l, 0, 0)),
                  pl.BlockSpec((None, ADA_RANK, bn), lambda l, j: (l, 0, j)),
                  pl.BlockSpec((None, 1, bn), lambda l, j: (l, 0, j))],
        out_specs=pl.BlockSpec((None, 8, bn), lambda l, j: (l, 0, j)),
        out_shape=jax.ShapeDtypeStruct((DEPTH, 8, N_MOD * D_MODEL), F32),
        compiler_params=_params(("arbitrary", "arbitrary"), 32),
        name="ada_up",
    )(t, ada_up, ada_b.reshape(DEPTH, 1, N_MOD * D_MODEL))


def _norm_mod_body(h_ref, g_ref, shift_ref, scale_ref, o_ref):
    bm = h_ref.shape[0]
    in_last = pl.program_id(0) == pl.num_programs(0) - 1
    gains = [g_ref[...] * (1.0 + scale_ref[k:k + 1, :]) for k in range(2)]
    shifts = [shift_ref[k:k + 1, :] for k in range(2)]
    for r0 in range(0, bm, NORM_ROWS):
        if r0 >= SEQ % bm:
            gain = jnp.where(in_last, gains[1], gains[0])
            shift = jnp.where(in_last, shifts[1], shifts[0])
        else:
            gain, shift = gains[0], shifts[0]
        x = h_ref[r0:r0 + NORM_ROWS, :]
        r = lax.rsqrt(jnp.mean(x * x, axis=-1, keepdims=True) + NORM_EPS)
        o_ref[r0:r0 + NORM_ROWS, :] = ((x * r) * gain + shift).astype(o_ref.dtype)


def _norm_mod(h, g_row, m, i_mod):
    return pl.pallas_call(
        _norm_mod_body,
        grid=(S_ALL // BM,),
        in_specs=[pl.BlockSpec((BM, D_MODEL), lambda i: (i, 0)),
                  pl.BlockSpec((1, D_MODEL), lambda i: (0, 0)),
                  pl.BlockSpec((8, D_MODEL), lambda i: (0, 3 * i_mod)),
                  pl.BlockSpec((8, D_MODEL), lambda i: (0, 3 * i_mod + 1))],
        out_specs=pl.BlockSpec((BM, D_MODEL), lambda i: (i, 0)),
        out_shape=jax.ShapeDtypeStruct((S_ALL, D_MODEL), BF16),
        compiler_params=_params(("arbitrary",), 48),
        name="norm_mod",
    )(h, g_row, m, m)


def _col_chunks(ref):
    return [slice(c, c + MXU_N) for c in range(0, ref.shape[1], MXU_N)]


def _swiglu_body(z_ref, wa_ref, wb_ref, o_ref):
    z = z_ref[...]
    for sl in _col_chunks(o_ref):
        a = _dot(z, wa_ref[:, sl])
        b = _dot(z, wb_ref[:, sl])
        o_ref[:, sl] = (a * _sigmoid(a) * b).astype(o_ref.dtype)


def _mm_swiglu(z, wi, jobs=()):
    bn = 512
    nj = D_FF // bn
    bm = BM_WIDE if all(j.nblk <= (S_ALL // BM_WIDE) * nj for j in jobs) else BM
    return _host_call(
        _swiglu_body, (S_ALL // bm, nj),
        [pl.BlockSpec((bm, D_MODEL), lambda i, j: (i, 0)),
         pl.BlockSpec((D_MODEL, bn), lambda i, j: (0, j)),
         pl.BlockSpec((D_MODEL, bn), lambda i, j: (0, j + nj))],
        pl.BlockSpec((bm, bn), lambda i, j: (i, j)),
        jax.ShapeDtypeStruct((S_ALL, D_FF), BF16),
        (z, wi, wi), jobs, 60, "mm_swiglu")


def _resid_body(x_ref, w_ref, h_ref, gate_ref, o_ref, *, coef):
    x = x_ref[...]
    bm = x_ref.shape[0]
    is_ctx = pl.program_id(0) * bm + lax.broadcasted_iota(jnp.int32, (bm, 1), 0) >= SEQ
    for sl in _col_chunks(o_ref):
        gate = jnp.where(is_ctx, gate_ref[1:2, sl], gate_ref[0:1, sl])
        o_ref[:, sl] = h_ref[:, sl] + (coef * gate) * _dot(x, w_ref[:, sl])


def _mm_resid(xin, w, h, m, i_mod, coef, jobs=(), latent_only=False):
    k = xin.shape[1]
    rows, bm = (SEQ, BM_LATENT) if latent_only else (S_ALL, BM_WIDE)
    bn = 512
    nj = D_MODEL // bn
    return _host_call(
        functools.partial(_resid_body, coef=coef), (rows // bm, nj),
        [pl.BlockSpec((bm, k), lambda i, j: (i, 0)),
         pl.BlockSpec((k, bn), lambda i, j: (0, j)),
         pl.BlockSpec((bm, bn), lambda i, j: (i, j)),
         pl.BlockSpec((8, bn), lambda i, j: (0, (3 * i_mod + 2) * nj + j))],
        pl.BlockSpec((bm, bn), lambda i, j: (i, j)),
        jax.ShapeDtypeStruct((rows, D_MODEL), F32),
        (xin, w, h, m), jobs, 60, "mm_resid")


def _plain_body(z_ref, w_ref, o_ref):
    z = z_ref[...]
    for sl in _col_chunks(o_ref):
        o_ref[:, sl] = _dot(z, w_ref[:, sl]).astype(o_ref.dtype)


def _mm_cols(z, w, col0, ncols, jobs=()):
    bm, bn = BM_WIDE, 512
    j0 = col0 // bn
    return _host_call(
        _plain_body, (S_ALL // bm, ncols // bn),
        [pl.BlockSpec((bm, D_MODEL), lambda i, j: (i, 0)),
         pl.BlockSpec((D_MODEL, bn), lambda i, j: (0, j0 + j))],
        pl.BlockSpec((bm, bn), lambda i, j: (i, j)),
        jax.ShapeDtypeStruct((S_ALL, ncols), BF16),
        (z, w), jobs, 56, "mm_cols")


def _merge_body(z_ref, y0, y1, y2, y3, g0, g1, g2, g3, b0, b1, b2, b3, o_ref):
    z = z_ref[...]
    acc = None
    for y_ref, g_ref, b_ref in ((y0, g0, b0), (y1, g1, b1), (y2, g2, b2), (y3, g3, b3)):
        t = _sigmoid(_dot(z, g_ref[...])) * _dot(y_ref[...], b_ref[...])
        acc = t if acc is None else acc + t
    o_ref[...] = acc.astype(o_ref.dtype)


def _merge(z, ys, w_gates, w_branch, jobs=()):
    bn = 256
    nj = D_MODEL // bn
    z_spec = pl.BlockSpec((BM, D_MODEL), lambda i, j: (i, 0))
    y_spec = pl.BlockSpec((BM, BRANCH_W), lambda i, j: (i, 0))
    g_specs = [pl.BlockSpec((D_MODEL, bn), functools.partial(lambda i, j, r: (0, r * nj + j), r=r))
               for r in range(N_BRANCH)]
    b_specs = [pl.BlockSpec((BRANCH_W, bn), functools.partial(lambda i, j, r: (r, j), r=r))
               for r in range(N_BRANCH)]
    return _host_call(
        _merge_body, (S_ALL // BM, nj),
        [z_spec] + [y_spec] * N_BRANCH + g_specs + b_specs,
        pl.BlockSpec((BM, bn), lambda i, j: (i, j)),
        jax.ShapeDtypeStruct((S_ALL, D_MODEL), BF16),
        (z, *ys, *([w_gates] * N_BRANCH), *([w_branch] * N_BRANCH)), jobs, 60, "merge")


def _head_norm_rope(t, g, cos, sin_lo, sin_hi):
    t = t * lax.rsqrt(jnp.mean(t * t, axis=-1, keepdims=True) + NORM_EPS) * g
    return t * cos + pltpu.roll(t, 96, 1) * sin_lo + pltpu.roll(t, 32, 1) * sin_hi


def _kprep_body(k_ref, g_ref, cos_ref, lo_ref, hi_ref, o_ref):
    for hk in range(N_KV_HEADS):
        sl = slice(hk * HEAD_DIM, (hk + 1) * HEAD_DIM)
        t = _head_norm_rope(k_ref[:, sl].astype(F32), g_ref[...], cos_ref[...], lo_ref[...], hi_ref[...])
        o_ref[:, sl] = t.astype(o_ref.dtype)


def _kprep(p, k_g, cos, sin_lo, sin_hi):
    tab = pl.BlockSpec((BS, HEAD_DIM), lambda i: (i, 0))
    return pl.pallas_call(
        _kprep_body,
        grid=(S_ALL // BS,),
        in_specs=[pl.BlockSpec((BS, KV_W), lambda i: (i, COL_K // KV_W)),
                  pl.BlockSpec((1, HEAD_DIM), lambda i: (0, 0)), tab, tab, tab],
        out_specs=pl.BlockSpec((BS, KV_W), lambda i: (i, 0)),
        out_shape=jax.ShapeDtypeStruct((S_ALL, KV_W), BF16),
        compiler_params=_params(("arbitrary",), 32),
        name="kprep",
    )(p, k_g, cos, sin_lo, sin_hi)


def _attn_body(sink_ref, q_ref, g_ref, cos_ref, lo_ref, hi_ref,
               kp_ref, kc_ref, kn_ref, kx_ref, vp_ref, vc_ref, vn_ref, vx_ref, o_ref):
    n = pl.program_id(0)
    nb_lat = SEQ // ATTN_BLOCK
    rows = GQA_GROUP * ATTN_BLOCK
    n_loc = 3 * ATTN_BLOCK
    is_lat = n < nb_lat
    lo = jnp.where(is_lat, jnp.where(n >= 1, 0, ATTN_BLOCK), n_loc)
    hi = jnp.where(n < nb_lat - 1, n_loc, 2 * ATTN_BLOCK)
    r = lax.broadcasted_iota(jnp.int32, (rows, n_loc), 0) & (ATTN_BLOCK - 1)
    a = lax.broadcasted_iota(jnp.int32, (rows, n_loc), 1)
    valid = (a >= r) & (a <= r + 2 * WINDOW) & (a >= lo) & (a < hi)
    row_head = lax.broadcasted_iota(jnp.int32, (rows, 1), 0) // ATTN_BLOCK
    scale = HEAD_DIM ** -0.5
    for hk in range(N_KV_HEADS):
        sl = slice(hk * HEAD_DIM, (hk + 1) * HEAD_DIM)
        qs = []
        for g in range(GQA_GROUP):
            h = hk * GQA_GROUP + g
            t = q_ref[:, h * HEAD_DIM:(h + 1) * HEAD_DIM].astype(F32)
            qs.append(_head_norm_rope(t, g_ref[...], cos_ref[...], lo_ref[...], hi_ref[...]).astype(BF16))
        q = jnp.concatenate(qs, axis=0)
        k_loc = jnp.concatenate([kp_ref[:, sl], kc_ref[:, sl], kn_ref[:, sl]], axis=0)
        v_loc = jnp.concatenate([vp_ref[:, sl], vc_ref[:, sl], vn_ref[:, sl]], axis=0)
        s_loc = jnp.where(valid, _dot_nt(q, k_loc) * scale, MASK_VALUE)
        s_ctx = _dot_nt(q, kx_ref[:, sl]) * scale
        snk = jnp.zeros((rows, 1), F32)
        for g in range(GQA_GROUP):
            snk = jnp.where(row_head == g, sink_ref[hk * GQA_GROUP + g], snk)
        m = jnp.maximum(jnp.maximum(jnp.max(s_loc, axis=-1, keepdims=True),
                                    jnp.max(s_ctx, axis=-1, keepdims=True)), snk)
        p_loc = jnp.exp(s_loc - m)
        p_ctx = jnp.exp(s_ctx - m)
        den = (jnp.sum(p_loc, axis=-1, keepdims=True) + jnp.sum(p_ctx, axis=-1, keepdims=True)
               + jnp.exp(snk - m))
        o = (_dot(p_loc.astype(BF16), v_loc) + _dot(p_ctx.astype(BF16), vx_ref[:, sl])) / den
        for g in range(GQA_GROUP):
            h = hk * GQA_GROUP + g
            o_ref[:, h * HEAD_DIM:(h + 1) * HEAD_DIM] = o[g * ATTN_BLOCK:(g + 1) * ATTN_BLOCK].astype(o_ref.dtype)


def _attention(p, k_rot, q_g, sink, cos, sin_lo, sin_hi):
    nb = S_ALL // ATTN_BLOCK
    blk_ctx = SEQ // CTX_LEN
    tab = pl.BlockSpec((ATTN_BLOCK, HEAD_DIM), lambda n: (n, 0))
    prev = lambda n: jnp.maximum(n - 1, 0)
    nxt = lambda n: jnp.minimum(n + 1, nb - 1)
    jv = COL_V // KV_W
    return pl.pallas_call(
        _attn_body,
        grid=(nb,),
        in_specs=[pl.BlockSpec(memory_space=pltpu.SMEM),
                  pl.BlockSpec((ATTN_BLOCK, Q_W), lambda n: (n, 0)),
                  pl.BlockSpec((1, HEAD_DIM), lambda n: (0, 0)), tab, tab, tab,
                  pl.BlockSpec((ATTN_BLOCK, KV_W), lambda n: (prev(n), 0)),
                  pl.BlockSpec((ATTN_BLOCK, KV_W), lambda n: (n, 0)),
                  pl.BlockSpec((ATTN_BLOCK, KV_W), lambda n: (nxt(n), 0)),
                  pl.BlockSpec((CTX_LEN, KV_W), lambda n: (blk_ctx, 0)),
                  pl.BlockSpec((ATTN_BLOCK, KV_W), lambda n: (prev(n), jv)),
                  pl.BlockSpec((ATTN_BLOCK, KV_W), lambda n: (n, jv)),
                  pl.BlockSpec((ATTN_BLOCK, KV_W), lambda n: (nxt(n), jv)),
                  pl.BlockSpec((CTX_LEN, KV_W), lambda n: (blk_ctx, jv))],
        out_specs=pl.BlockSpec((ATTN_BLOCK, Q_W), lambda n: (n, 0)),
        out_shape=jax.ShapeDtypeStruct((S_ALL, Q_W), BF16),
        compiler_params=_params(("arbitrary",), 32),
        name="attention",
    )(sink, p, q_g, cos, sin_lo, sin_hi, k_rot, k_rot, k_rot, k_rot, p, p, p, p)


def _sgu_body(u_ref, v_ref, lg_ref, lb_ref, ws_ref, bs_ref, o_ref):
    half = u_ref.shape[1] // SGU_CH
    for g in range(half):
        sl = slice(g * SGU_CH, (g + 1) * SGU_CH)
        vg = jax.nn.gelu(v_ref[:, sl].astype(F32))
        mu = jnp.mean(vg, axis=-1, keepdims=True)
        d = vg - mu
        var = jnp.mean(d * d, axis=-1, keepdims=True)
        vn = (d * lax.rsqrt(var + NORM_EPS) * lg_ref[:, sl] + lb_ref[:, sl]).astype(BF16)
        for c in range(u_ref.shape[0] // CHUNK):
            rs = slice(c * CHUNK, (c + 1) * CHUNK)
            mixed = _dot(ws_ref[g], vn[rs]) + bs_ref[:, g:g + 1]
            o_ref[rs, sl] = (jax.nn.gelu(u_ref[rs, sl].astype(F32)) * mixed).astype(o_ref.dtype)


def _sgu(p, ln_g, ln_b, w_s, b_s_t):
    bw = 512
    gh = bw // SGU_CH
    ju, jv = COL_SGU_U // bw, COL_SGU_V // bw
    return pl.pallas_call(
        _sgu_body,
        grid=(S_ALL // BS, SGU_W // bw),
        in_specs=[pl.BlockSpec((BS, bw), lambda i, c: (i, ju + c)),
                  pl.BlockSpec((BS, bw), lambda i, c: (i, jv + c)),
                  pl.BlockSpec((1, bw), lambda i, c: (0, c)),
                  pl.BlockSpec((1, bw), lambda i, c: (0, c)),
                  pl.BlockSpec((gh, CHUNK, CHUNK), lambda i, c: (c, 0, 0)),
                  pl.BlockSpec((None, CHUNK, gh), lambda i, c: (c, 0, 0))],
        out_specs=pl.BlockSpec((BS, bw), lambda i, c: (i, c)),
        out_shape=jax.ShapeDtypeStruct((S_ALL, SGU_W), BF16),
        compiler_params=_params(("arbitrary", "arbitrary"), 32),
        name="sgu",
    )(p, p, ln_g, ln_b, w_s, b_s_t)


HALO = 16


def _conv_body(x_ref, b_ref, c_ref, xp_ref, cp_ref, xn_ref, cn_ref, w_ref, o_ref):
    bs = x_ref.shape[0]
    i = pl.program_id(0)
    in_last = i == pl.num_programs(0) - 1
    z = c_ref[...].astype(F32) * x_ref[...].astype(F32)
    zp = cp_ref[HALO - 1:HALO, :].astype(F32) * xp_ref[HALO - 1:HALO, :].astype(F32)
    zn = cn_ref[0:1, :].astype(F32) * xn_ref[0:1, :].astype(F32)
    zp = jnp.where(i != 0, zp, 0.0)
    zn = jnp.where(in_last, 0.0, zn)
    row = lax.broadcasted_iota(jnp.int32, (bs, 1), 0)
    z_prev = jnp.where(row == 0, zp, pltpu.roll(z, 1, 0))
    z_next = jnp.where(row == bs - 1, zn, pltpu.roll(z, bs - 1, 0))
    z_prev = jnp.where(in_last & (row == SEQ % bs), 0.0, z_prev)
    z_next = jnp.where(in_last & (row == SEQ % bs - 1), 0.0, z_next)
    y = w_ref[0:1, :] * z_prev + w_ref[1:2, :] * z + w_ref[2:3, :] * z_next
    o_ref[...] = (b_ref[...].astype(F32) * y).astype(o_ref.dtype)


def _short_conv(p, conv_w):
    bw = 512
    jx, jb, jc = COL_CONV_X // bw, COL_CONV_B // bw, COL_CONV_C // bw
    rb = BS // HALO
    nh = S_ALL // HALO
    prev = lambda i: jnp.maximum(i * rb - 1, 0)
    nxt = lambda i: jnp.minimum((i + 1) * rb, nh - 1)
    return pl.pallas_call(
        _conv_body,
        grid=(S_ALL // BS, CONV_W // bw),
        in_specs=[pl.BlockSpec((BS, bw), lambda i, c: (i, jx + c)),
                  pl.BlockSpec((BS, bw), lambda i, c: (i, jb + c)),
                  pl.BlockSpec((BS, bw), lambda i, c: (i, jc + c)),
                  pl.BlockSpec((HALO, bw), lambda i, c: (prev(i), jx + c)),
                  pl.BlockSpec((HALO, bw), lambda i, c: (prev(i), jc + c)),
                  pl.BlockSpec((HALO, bw), lambda i, c: (nxt(i), jx + c)),
                  pl.BlockSpec((HALO, bw), lambda i, c: (nxt(i), jc + c)),
                  pl.BlockSpec((3, bw), lambda i, c: (0, c))],
        out_specs=pl.BlockSpec((BS, bw), lambda i, c: (i, c)),
        out_shape=jax.ShapeDtypeStruct((S_ALL, CONV_W), BF16),
        compiler_params=_params(("arbitrary", "arbitrary"), 32),
        name="short_conv",
    )(p, p, p, p, p, p, p, conv_w)


def _dft_consts():
    def cs(n, rows, cols):
        ang = 2.0 * np.pi * ((np.outer(rows, cols)) % n) / n
        return np.cos(ang), np.sin(ang)

    t1, t2 = DFT_T1, DFT_T2
    c1, s1 = cs(t1, np.arange(t1), np.arange(t1))
    m1 = np.concatenate([c1, -s1], axis=0)
    cw, sw = cs(SEQ, np.arange(t1), np.arange(t2))
    c2, s2 = cs(t2, np.arange(t2), np.arange(t2))
    m2 = np.block([[c2, s2], [-s2, c2]])
    cx, sx = cs(CTX_LEN, np.arange(CTX_LEN), np.arange(CTX_LEN))
    m2x = np.block([[cx, sx], [-sx, cx]])
    cc, sc = cs(FOURIER_CH, np.arange(FOURIER_CH), np.arange(FOURIER_CH))
    nl = 1.0 / np.sqrt(SEQ * FOURIER_CH)
    nx = 1.0 / np.sqrt(CTX_LEN * FOURIER_CH)
    as_bf = lambda a: jnp.asarray(a, dtype=F32).astype(BF16)
    return dict(
        m1=as_bf(m1), m2=as_bf(m2), m2x=as_bf(m2x),
        cw=jnp.asarray(cw[:, :, None], F32), sw=jnp.asarray(sw[:, :, None], F32),
        cwx=jnp.ones((1, CTX_LEN, 1), F32), swx=jnp.zeros((1, CTX_LEN, 1), F32),
        cc_l=as_bf(cc * nl), sc_l=as_bf(sc * nl), cc_x=as_bf(cc * nx), sc_x=as_bf(sc * nx))


def _dft1_body(m_ref, x_ref, o_ref):
    o_ref[...] = _dot(m_ref[...], x_ref[...]).astype(o_ref.dtype)


def _dft_stage1(f, m1):
    ncol = DFT_T2 * FOURIER_W
    bn = 8192
    fv = f.reshape(S_ALL // DFT_T2, ncol)
    return pl.pallas_call(
        _dft1_body,
        grid=(ncol // bn,),
        in_specs=[pl.BlockSpec((2 * DFT_T1, DFT_T1), lambda j: (0, 0)),
                  pl.BlockSpec((DFT_T1, bn), lambda j: (0, j))],
        out_specs=pl.BlockSpec((2 * DFT_T1, bn), lambda j: (0, j)),
        out_shape=jax.ShapeDtypeStruct((2 * DFT_T1, ncol), BF16),
        compiler_params=_params(("arbitrary",), 32),
        name="dft_stage1",
    )(m1, fv)


def _dft2_body(ar_ref, ai_ref, cw_ref, sw_ref, m2_ref, cc_ref, sc_ref, o_ref):
    n_grp, t2 = ar_ref.shape[0], ar_ref.shape[1]
    for k in range(n_grp):
        ar = ar_ref[k].astype(F32)
        ai = ai_ref[k].astype(F32)
        cw = cw_ref[k]
        sw = sw_ref[k]
        b = jnp.concatenate([ar * cw + ai * sw, ai * cw - ar * sw], axis=0).astype(BF16)
        p = _dot(m2_ref[...], b)
        pr = p[:t2].astype(BF16)
        pi = p[t2:].astype(BF16)
        for g in range(FOURIER_GROUPS):
            sl = slice(g * FOURIER_CH, (g + 1) * FOURIER_CH)
            y = _dot(pr[:, sl], cc_ref[...]) + _dot(pi[:, sl], sc_ref[...])
            o_ref[:, k * FOURIER_W + g * FOURIER_CH:k * FOURIER_W + (g + 1) * FOURIER_CH] = y.astype(o_ref.dtype)


def _dft_stage2(a3, n_ka, t2, im_off, cw, sw, m2, cc, sc):
    grp = min(n_ka, 4)
    io = im_off // grp
    return pl.pallas_call(
        _dft2_body,
        grid=(n_ka // grp,),
        in_specs=[pl.BlockSpec((grp, t2, FOURIER_W), lambda ka: (ka, 0, 0)),
                  pl.BlockSpec((grp, t2, FOURIER_W), lambda ka: (io + ka, 0, 0)),
                  pl.BlockSpec((grp, t2, 1), lambda ka: (ka, 0, 0)),
                  pl.BlockSpec((grp, t2, 1), lambda ka: (ka, 0, 0)),
                  pl.BlockSpec((2 * t2, 2 * t2), lambda ka: (0, 0)),
                  pl.BlockSpec((FOURIER_CH, FOURIER_CH), lambda ka: (0, 0)),
                  pl.BlockSpec((FOURIER_CH, FOURIER_CH), lambda ka: (0, 0))],
        out_specs=pl.BlockSpec((t2, grp * FOURIER_W), lambda ka: (0, ka)),
        out_shape=jax.ShapeDtypeStruct((t2, n_ka * FOURIER_W), BF16),
        compiler_params=_params(("arbitrary",), 32),
        name="dft_stage2",
    )(a3, a3, cw, sw, m2, cc, sc)


def _fourier(f, k):
    a = _dft_stage1(f, k["m1"]).reshape(2 * DFT_T1, DFT_T2, FOURIER_W)
    y_lat = _dft_stage2(a, DFT_T1, DFT_T2, DFT_T1, k["cw"], k["sw"], k["m2"], k["cc_l"], k["sc_l"])
    y_lat = y_lat.reshape(SEQ, FOURIER_W)
    fx = jnp.stack([f[SEQ:], jnp.zeros((CTX_LEN, FOURIER_W), f.dtype)])
    y_ctx = _dft_stage2(fx, 1, CTX_LEN, 1, k["cwx"], k["swx"], k["m2x"], k["cc_x"], k["sc_x"])
    return jnp.concatenate([y_lat, y_ctx], axis=0)


def _rope_tables():
    pos = np.arange(SEQ)
    ax = HEAD_DIM // 2
    inv_freq = 1.0 / (ROPE_THETA ** (np.arange(0, ax, 2, dtype=np.float32) / ax)).astype(np.float32)
    ang_r = (pos // GRID_W).astype(np.float32)[:, None] * inv_freq[None, :]
    ang_c = (pos % GRID_W).astype(np.float32)[:, None] * inv_freq[None, :]
    emb = np.concatenate([ang_r, ang_r, ang_c, ang_c], axis=-1).astype(np.float32)
    cos = np.cos(emb.astype(np.float64))
    sin = np.sin(emb.astype(np.float64))
    first = (np.arange(HEAD_DIM) % (HEAD_DIM // 2)) < HEAD_DIM // 4
    sin_lo = np.where(first[None, :], -sin, 0.0)
    sin_hi = np.where(first[None, :], 0.0, sin)
    pad = lambda a, v: np.concatenate([a, np.full((CTX_LEN, HEAD_DIM), v)], axis=0)
    return (jnp.asarray(pad(cos, 1.0), F32), jnp.asarray(pad(sin_lo, 0.0), F32),
            jnp.asarray(pad(sin_hi, 0.0), F32))


def kernel(x, c, ctx, c_ctx, ada_down, ada_up, ada_b, norm_g, ffn_wi, ffn_wo, w_in, q_norm, k_norm,
           sink, sgu_ln_g, sgu_ln_b, sgu_w, sgu_b, conv_w, w_branch, w_out):
    assert x.shape == (1, SEQ, D_MODEL) and ctx.shape == (1, CTX_LEN, D_MODEL)
    h = jnp.concatenate([x[0], ctx[0]], axis=0)
    cv = jnp.concatenate([c, c_ctx[None, :], jnp.zeros((6, D_MODEL), F32)], axis=0)
    m_all = _ada_mod(cv, ada_down, ada_up, ada_b)
    cos, sin_lo, sin_hi = _rope_tables()
    dft = _dft_consts()

    wb4 = w_branch.reshape(DEPTH, N_BRANCH * BRANCH_W, D_MODEL)

    def job(l, name):
        big = (512, 1024)
        gate_rb = 2048 if l == 0 else 1024
        wi1_rb = 512 if l == 0 else 1024
        return dict(
            wi0=lambda: _CastJob(ffn_wi, (l, 0), D_MODEL, 0, 2 * D_FF, *big),
            wi1=lambda: _CastJob(ffn_wi, (l, 1), D_MODEL, 0, 2 * D_FF, wi1_rb, 1024),
            wo0=lambda: _CastJob(ffn_wo, (l, 0), D_FF, 0, D_MODEL, *big),
            wo1=lambda: _CastJob(ffn_wo, (l, 1), D_FF, 0, D_MODEL, *big),
            wic=lambda: _CastJob(w_in, (l,), D_MODEL, 0, COL_GATES, 1024, 512),
            wig=lambda: _CastJob(w_in, (l,), D_MODEL, COL_GATES, N_BRANCH * D_MODEL, gate_rb, 512),
            wb=lambda: _CastJob(wb4, (l,), N_BRANCH * BRANCH_W, 0, D_MODEL, *big),
            wout=lambda: _CastJob(w_out, (l,), D_MODEL, 0, D_MODEL, *big))[name]()

    def rides(call, l):
        nxt = {"swiglu1": [], "resid1": ["wout", "wo0"], "cols": ["wic", "wi0"], "merge": ["wig"],
               "resid_out": ["wb", "wo1"], "swiglu2": [], "resid2": ["wi1"]}[call]
        keys = [(l + 1, n) for n in nxt] if l + 1 < DEPTH else []
        if l == 0:
            keys += [(0, n) for n in {"swiglu1": ["wig", "wi1", "wo1"], "resid1": ["wout"],
                                      "cols": ["wb"]}.get(call, [])]
        return keys

    wts = {(0, n): _cast_now(job(0, n)) for n in ("wi0", "wo0", "wic")}
    for l in range(DEPTH):
        m = m_all[l]

        def run(call, fn):
            keys = rides(call, l)
            out, cast = fn([job(*k) for k in keys])
            wts.update(zip(keys, cast))
            return out

        z = _norm_mod(h, norm_g[l, 0][None, :], m, 0)
        g = run("swiglu1", lambda jb: _mm_swiglu(z, wts[l, "wi0"], jb))
        h = run("resid1", lambda jb: _mm_resid(g, wts[l, "wo0"], h, m, 0, 0.5, jb))
        z = _norm_mod(h, norm_g[l, 1][None, :], m, 1)
        p = run("cols", lambda jb: _mm_cols(z, wts[l, "wic"], 0, COL_FOURIER, jb))
        f, _ = _mm_cols(z, wts[l, "wic"], COL_FOURIER, FOURIER_W)
        k_rot = _kprep(p, k_norm[l][None, :], cos, sin_lo, sin_hi)
        ys = (_attention(p, k_rot, q_norm[l][None, :], sink[l], cos, sin_lo, sin_hi),
              _sgu(p, sgu_ln_g[l][None, :], sgu_ln_b[l][None, :], sgu_w[l].astype(BF16),
                   sgu_b[l].reshape(SGU_W // 512, 512 // SGU_CH, CHUNK).transpose(0, 2, 1)),
              _short_conv(p, conv_w[l]),
              _fourier(f, dft))
        merged = run("merge", lambda jb: _merge(z, ys, wts[l, "wig"], wts[l, "wb"], jb))
        h = run("resid_out", lambda jb: _mm_resid(merged, wts[l, "wout"], h, m, 1, 1.0, jb))
        z = _norm_mod(h, norm_g[l, 2][None, :], m, 2)
        g = run("swiglu2", lambda jb: _mm_swiglu(z, wts[l, "wi1"], jb))
        h = run("resid2", lambda jb: _mm_resid(g, wts[l, "wo1"], h, m, 2, 0.5, jb,
                                               latent_only=l == DEPTH - 1))
    return h[None]
```
